```python
import jax
import jax.numpy as jnp
from jax import lax
import numpy as np

D_MODEL = 1024
BATCH = 8
SEQ = 4096
DEPTH = 1

N_MEM = 256
ROPE_THETA = 500000.0
NORM_EPS = 1e-6
NEG_INF = -1e30
Q_BLOCK = 128
N_BRANCHES = 3

MLA_HEADS = 8
MLA_Q_RANK = 384
MLA_KV_RANK = 128
MLA_NOPE_DIM = 64
MLA_ROPE_DIM = 32
MLA_V_DIM = 64
MLA_QK_DIM = MLA_NOPE_DIM + MLA_ROPE_DIM

DIL_GROUPS = ((128, 1), (512, 4), (2048, 16))
DIL_HEADS = 4
DIL_HEAD_DIM = 64
DIL_ROPE_DIM = DIL_HEAD_DIM // 4
DIL_WIDTH = len(DIL_GROUPS) * DIL_HEADS * DIL_HEAD_DIM

MEM_HEADS = 4
MEM_HEAD_DIM = 128
MEM_WIDTH = MEM_HEADS * MEM_HEAD_DIM

D_FF = 2816

IN_SPLITS = (MLA_Q_RANK, MLA_KV_RANK, MLA_ROPE_DIM,
             DIL_WIDTH, DIL_WIDTH, DIL_WIDTH,
             MEM_WIDTH, N_BRANCHES * D_MODEL)
D_IN = sum(IN_SPLITS)

kernel_name = 'hybrid_mla_dilated_mem_encoder_block'


def rms_norm(x, g):
    xf = x.astype(jnp.float32)
    y = xf * lax.rsqrt(jnp.mean(xf * xf, axis=-1, keepdims=True) + NORM_EPS)
    return (y * g.astype(jnp.float32)).astype(x.dtype)


def swiglu(h, w_gate, w_up, w_down):
    return (jax.nn.silu(h @ w_gate) * (h @ w_up)) @ w_down


def split_columns(z, sizes):
    outs, start = [], 0
    for n in sizes:
        outs.append(z[..., start:start + n])
        start += n
    return outs


def rope(x, pos, rot_dim):
    half = rot_dim // 2
    inv_freq = ROPE_THETA ** (-2.0 * jnp.arange(half, dtype=jnp.float32) / rot_dim)
    ang = pos.astype(jnp.float32)[..., None] * inv_freq
    cos = jnp.cos(ang)[:, :, None, :]
    sin = jnp.sin(ang)[:, :, None, :]
    xr = x[..., :rot_dim].astype(jnp.float32)
    x1, x2 = xr[..., :half], xr[..., half:]
    rot = jnp.concatenate([x1 * cos - x2 * sin, x2 * cos + x1 * sin], axis=-1).astype(x.dtype)
    return jnp.concatenate([rot, x[..., rot_dim:]], axis=-1)


def dense_block_attention(q, k, v):
    B, S, H, dq = q.shape
    nq = S // Q_BLOCK
    scale = dq ** -0.5
    qb = q.reshape(B, nq, Q_BLOCK, H, dq).transpose(1, 0, 2, 3, 4)

    def attend(q_blk):
        s = jnp.einsum('bqhc,bkhc->bhqk', q_blk, k).astype(jnp.float32) * scale
        p = jax.nn.softmax(s, axis=-1).astype(v.dtype)
        return jnp.einsum('bhqk,bkhc->bqhc', p, v)

    o = lax.map(attend, qb)
    return o.transpose(1, 0, 2, 3, 4).reshape(B, S, H, v.shape[-1])


def dilated_window_attention(q, k, v, dilation, n_side):
    B, S, H, dh = q.shape
    L = S // dilation
    blk = n_side
    nb = -(-L // blk)
    Lp = nb * blk

    def classes(t):
        t = t.reshape(B, L, dilation, H, dh).transpose(0, 2, 1, 3, 4)
        return jnp.pad(t, ((0, 0), (0, 0), (0, Lp - L), (0, 0), (0, 0)))

    def windows(t):
        tp = jnp.pad(t, ((0, 0), (0, 0), (blk, blk), (0, 0), (0, 0)))
        tp = tp.reshape(B, dilation, nb + 2, blk, H, dh)
        return jnp.concatenate([tp[:, :, :-2], tp[:, :, 1:-1], tp[:, :, 2:]], axis=3)

    qb = classes(q).reshape(B, dilation, nb, blk, H, dh)
    kw = windows(classes(k))
    vw = windows(classes(v))

    q_idx = jnp.arange(Lp).reshape(nb, blk)
    k_idx = jnp.arange(nb)[:, None] * blk - blk + jnp.arange(3 * blk)[None, :]
    rel = k_idx[:, None, :] - q_idx[:, :, None]
    mask = (jnp.abs(rel) <= n_side) & (k_idx[:, None, :] >= 0) & (k_idx[:, None, :] < L)

    s = jnp.einsum('bdnqhc,bdnkhc->bdnhqk', qb, kw).astype(jnp.float32) * dh ** -0.5
    s = jnp.where(mask[:, None], s, NEG_INF)
    lse = jax.nn.logsumexp(s, axis=-1)
    p = jnp.exp(s - lse[..., None]).astype(v.dtype)
    o = jnp.einsum('bdnhqk,bdnkhc->bdnqhc', p, vw)
    o = o.reshape(B, dilation, Lp, H, dh)[:, :, :L].transpose(0, 2, 1, 3, 4).reshape(B, S, H, dh)
    lse = lse.transpose(0, 1, 2, 4, 3).reshape(B, dilation, Lp, H)[:, :, :L]
    lse = lse.transpose(0, 2, 1, 3).reshape(B, S, H)
    return o, lse


def mla_branch(c_q, c_kv, k_r, pos, q_norm, w_uq, kv_norm, w_ukv, w_o):
    B, S, _ = c_q.shape
    q = (rms_norm(c_q, q_norm) @ w_uq).reshape(B, S, MLA_HEADS, MLA_QK_DIM)
    q = jnp.concatenate([q[..., :MLA_NOPE_DIM],
                         rope(q[..., MLA_NOPE_DIM:], pos, MLA_ROPE_DIM)], axis=-1)
    kv = (rms_norm(c_kv, kv_norm) @ w_ukv).reshape(B, S, MLA_HEADS, MLA_NOPE_DIM + MLA_V_DIM)
    k_rope = rope(k_r[:, :, None, :], pos, MLA_ROPE_DIM)
    k = jnp.concatenate([kv[..., :MLA_NOPE_DIM],
                         jnp.broadcast_to(k_rope, (B, S, MLA_HEADS, MLA_ROPE_DIM))], axis=-1)
    v = kv[..., MLA_NOPE_DIM:]
    o = dense_block_attention(q, k, v)
    return o.reshape(B, S, MLA_HEADS * MLA_V_DIM) @ w_o


def dilated_branch(q, k, v, pos, w_o):
    B, S, _ = q.shape
    G = len(DIL_GROUPS)

    def heads(t):
        t = t.reshape(B, S, G * DIL_HEADS, DIL_HEAD_DIM)
        return t

    qh = rope(heads(q), pos, DIL_ROPE_DIM).reshape(B, S, G, DIL_HEADS, DIL_HEAD_DIM)
    kh = rope(heads(k), pos, DIL_ROPE_DIM).reshape(B, S, G, DIL_HEADS, DIL_HEAD_DIM)
    vh = v.reshape(B, S, G, DIL_HEADS, DIL_HEAD_DIM)
    outs, lses = [], []
    for g, (window, dilation) in enumerate(DIL_GROUPS):
        n_side = window // (2 * dilation)
        o, lse = dilated_window_attention(qh[:, :, g], kh[:, :, g], vh[:, :, g], dilation, n_side)
        outs.append(o)
        lses.append(lse)
    alpha = jax.nn.softmax(jnp.stack(lses, axis=0), axis=0)
    o = jnp.sum(alpha[..., None] * jnp.stack(outs, axis=0).astype(jnp.float32), axis=0)
    return o.astype(q.dtype).reshape(B, S, DIL_HEADS * DIL_HEAD_DIM) @ w_o


def memory_branch(q, mem, mem_norm, w_kv, w_o):
    B, S, _ = q.shape
    M = mem.shape[1]
    kv = rms_norm(mem, mem_norm) @ w_kv
    km = kv[..., :MEM_WIDTH].reshape(B, M, MEM_HEADS, MEM_HEAD_DIM)
    vm = kv[..., MEM_WIDTH:].reshape(B, M, MEM_HEADS, MEM_HEAD_DIM)
    qh = q.reshape(B, S, MEM_HEADS, MEM_HEAD_DIM)
    s = jnp.einsum('bshc,bmhc->bhsm', qh, km).astype(jnp.float32) * MEM_HEAD_DIM ** -0.5
    p = jax.nn.softmax(s, axis=-1).astype(vm.dtype)
    o = jnp.einsum('bhsm,bmhc->bshc', p, vm)
    return o.reshape(B, S, MEM_WIDTH) @ w_o


def setup_inputs(seed: int = 0) -> dict:
    key = jax.random.key(seed)
    ks = jax.random.split(key, 24)
    f32 = jnp.float32

    def dense(k, shape):
        return jax.random.normal(k, shape, f32) * shape[-2] ** -0.5

    def gain(k, n):
        return 1.0 + 0.02 * jax.random.normal(k, (DEPTH, n), f32)

    offsets = jax.random.randint(ks[2], (BATCH, 1), 0, 1024, dtype=jnp.int32)
    positions = jnp.arange(SEQ, dtype=jnp.int32)[None, :] + offsets
    return {
        'x': jax.random.normal(ks[0], (BATCH, SEQ, D_MODEL), f32),
        'mem': jax.random.normal(ks[1], (BATCH, N_MEM, D_MODEL), f32),
        'positions': positions,
        'ffn1_norm': gain(ks[3], D_MODEL),
        'ffn1_w_gate': dense(ks[4], (DEPTH, D_MODEL, D_FF)),
        'ffn1_w_up': dense(ks[5], (DEPTH, D_MODEL, D_FF)),
        'ffn1_w_down': dense(ks[6], (DEPTH, D_FF, D_MODEL)),
        'mix_norm': gain(ks[7], D_MODEL),
        'w_in': dense(ks[8], (DEPTH, D_MODEL, D_IN)),
        'mla_q_norm': gain(ks[9], MLA_Q_RANK),
        'mla_w_uq': dense(ks[10], (DEPTH, MLA_Q_RANK, MLA_HEADS * MLA_QK_DIM)),
        'mla_kv_norm': gain(ks[11], MLA_KV_RANK),
        'mla_w_ukv': dense(ks[12], (DEPTH, MLA_KV_RANK, MLA_HEADS * (MLA_NOPE_DIM + MLA_V_DIM))),
        'mla_w_o': dense(ks[13], (DEPTH, MLA_HEADS * MLA_V_DIM, D_MODEL)),
        'dil_w_o': dense(ks[14], (DEPTH, DIL_HEADS * DIL_HEAD_DIM, D_MODEL)),
        'mem_norm': gain(ks[15], D_MODEL),
        'mem_w_kv': dense(ks[16], (DEPTH, D_MODEL, 2 * MEM_WIDTH)),
        'mem_w_o': dense(ks[17], (DEPTH, MEM_WIDTH, D_MODEL)),
        'w_out': dense(ks[18], (DEPTH, D_MODEL, D_MODEL)),
        'ffn2_norm': gain(ks[19], D_MODEL),
        'ffn2_w_gate': dense(ks[20], (DEPTH, D_MODEL, D_FF)),
        'ffn2_w_up': dense(ks[21], (DEPTH, D_MODEL, D_FF)),
        'ffn2_w_down': dense(ks[22], (DEPTH, D_FF, D_MODEL)),
        'final_norm': 1.0 + 0.02 * jax.random.normal(ks[23], (D_MODEL,), f32),
    }


def reference(x, mem, positions, ffn1_norm, ffn1_w_gate, ffn1_w_up, ffn1_w_down,
              mix_norm, w_in, mla_q_norm, mla_w_uq, mla_kv_norm, mla_w_ukv, mla_w_o,
              dil_w_o, mem_norm, mem_w_kv, mem_w_o, w_out,
              ffn2_norm, ffn2_w_gate, ffn2_w_up, ffn2_w_down, final_norm):
    B, S, _ = x.shape
    for l in range(DEPTH):
        x = x + 0.5 * swiglu(rms_norm(x, ffn1_norm[l]), ffn1_w_gate[l], ffn1_w_up[l], ffn1_w_down[l])

        h = rms_norm(x, mix_norm[l])
        z = h @ w_in[l]
        c_q, c_kv, k_r, dq, dk, dv, mq, gate_logits = split_columns(z, IN_SPLITS)

        y_mla = mla_branch(c_q, c_kv, k_r, positions, mla_q_norm[l], mla_w_uq[l],
                           mla_kv_norm[l], mla_w_ukv[l], mla_w_o[l])
        y_dil = dilated_branch(dq, dk, dv, positions, dil_w_o[l])
        y_mem = memory_branch(mq, mem, mem_norm[l], mem_w_kv[l], mem_w_o[l])

        gates = jax.nn.sigmoid(gate_logits.reshape(B, S, N_BRANCHES, D_MODEL))
        mixed = gates[:, :, 0] * y_mla + gates[:, :, 1] * y_dil + gates[:, :, 2] * y_mem
        x = x + mixed @ w_out[l]

        x = x + 0.5 * swiglu(rms_norm(x, ffn2_norm[l]), ffn2_w_gate[l], ffn2_w_up[l], ffn2_w_down[l])
    return rms_norm(x, final_norm)
```

```python
import functools

import jax
import jax.numpy as jnp
from jax import lax
from jax.experimental import pallas as pl
from jax.experimental.pallas import tpu as pltpu

F32 = jnp.float32
BF16 = jnp.bfloat16

NORM_EPS = 1e-6
NEG_INF = -1e30
ROPE_THETA = 500000.0

LANES = 128
VMEM_LIMIT_BYTES = 56 * 1024 * 1024

MLA_HEADS = 8
MLA_Q_RANK = 384
MLA_KV_RANK = 128
MLA_NOPE_DIM = 64
MLA_ROPE_DIM = 32
MLA_V_DIM = 64
MLA_QK_DIM = MLA_NOPE_DIM + MLA_ROPE_DIM
MLA_HEAD_PAD = 128

DIL_GROUPS = ((128, 1), (512, 4), (2048, 16))
DIL_HEADS = 4
DIL_HEAD_DIM = 64
DIL_ROPE_DIM = 16
DIL_GROUP_WIDTH = DIL_HEADS * DIL_HEAD_DIM
DIL_WIDTH = len(DIL_GROUPS) * DIL_GROUP_WIDTH
DIL_Q_TILE = 128
DIL_K_TILE = 256

MEM_HEADS = 4
MEM_HEAD_DIM = 128
MEM_WIDTH = MEM_HEADS * MEM_HEAD_DIM

N_BRANCHES = 3


def _dot(a, b):
    return jnp.dot(a, b, preferred_element_type=F32)


def _dot_nt(a, b):
    return lax.dot_general(a, b, (((1,), (1,)), ((), ())), preferred_element_type=F32)


def _rms(x, g):
    y = x * lax.rsqrt(jnp.mean(x * x, axis=-1, keepdims=True) + NORM_EPS)
    return y * g


def _const_spec(shape):
    nd = len(shape)
    return pl.BlockSpec(shape, lambda *_: (0,) * nd, pipeline_mode=pl.Buffered(1))


def _params(n_grid):
    return pltpu.CompilerParams(
        dimension_semantics=("arbitrary",) * n_grid,
        vmem_limit_bytes=VMEM_LIMIT_BYTES,
    )


def _ffn_kernel(x_ref, g_ref, wg_ref, wu_ref, wd_ref, *rest, f_chunk, final):
    if final:
        fg_ref, o_ref = rest
    else:
        (o_ref,) = rest
    x = x_ref[...]
    h = _rms(x, g_ref[...]).astype(BF16)
    d_ff = wg_ref.shape[1]
    acc = jnp.zeros(x.shape, F32)
    for c in range(d_ff // f_chunk):
        cols = slice(c * f_chunk, (c + 1) * f_chunk)
        a = _dot(h, wg_ref[:, cols])
        b = _dot(h, wu_ref[:, cols])
        act = (jax.nn.silu(a) * b).astype(BF16)
        acc = acc + _dot(act, wd_ref[cols, :])
    y = x + 0.5 * acc
    if final:
        y = _rms(y, fg_ref[...])
    o_ref[...] = y


def _ffn(x2d, g, wg, wu, wd, final_g=None, *, tm=512, f_chunk=256):
    t, d = x2d.shape
    d_ff = wg.shape[1]
    assert t % tm == 0 and d_ff % f_chunk == 0
    final = final_g is not None
    in_specs = [
        pl.BlockSpec((tm, d), lambda i: (i, 0)),
        _const_spec((1, d)),
        _const_spec((d, d_ff)),
        _const_spec((d, d_ff)),
        _const_spec((d_ff, d)),
    ]
    args = [x2d, g, wg, wu, wd]
    if final:
        in_specs.append(_const_spec((1, d)))
        args.append(final_g)
    return pl.pallas_call(
        functools.partial(_ffn_kernel, f_chunk=f_chunk, final=final),
        grid=(t // tm,),
        in_specs=in_specs,
        out_specs=pl.BlockSpec((tm, d), lambda i: (i, 0)),
        out_shape=jax.ShapeDtypeStruct((t, d), F32),
        compiler_params=_params(1),
        name="ffn_final" if final else "ffn",
    )(*args)


def _rope_tables(pos, invf, m_fwd, m_bwd):
    ang = pos * invf
    c = jnp.cos(ang)
    s = jnp.sin(ang)
    return c, s * m_fwd, s * m_bwd


def _rope_apply(x, half, c, s_fwd, s_bwd):
    return (x * c
            + pltpu.roll(x, half, axis=1) * s_fwd
            + pltpu.roll(x, LANES - half, axis=1) * s_bwd)


def _mixproj_kernel(x_ref, pos_ref, g_ref, w_ref, qn_ref, wuq_ref, kvn_ref, wukv_ref, rc_ref,
                    qm_ref, km_ref, vm_ref,
                    q0_ref, k0_ref, v0_ref, q1_ref, k1_ref, v1_ref, q2_ref, k2_ref, v2_ref,
                    mq_ref, zd_ref):
    tm = x_ref.shape[0]
    h = _rms(x_ref[...], g_ref[...]).astype(BF16)
    pos = pos_ref[...]
    rc = rc_ref[...]

    c_m, sf_m, sb_m = _rope_tables(pos, rc[0:1], rc[1:2], rc[2:3])
    q_scale = MLA_QK_DIM ** -0.5
    cq_m, sfq_m, sbq_m = c_m * q_scale, sf_m * q_scale, sb_m * q_scale
    half_m = MLA_ROPE_DIM // 2

    z = _dot(h, w_ref[:, 0:640])
    cq = _rms(z[:, 0:MLA_Q_RANK], qn_ref[...]).astype(BF16)
    ckv = _rms(z[:, MLA_Q_RANK:MLA_Q_RANK + MLA_KV_RANK], kvn_ref[...]).astype(BF16)
    kr = _rope_apply(z[:, 512:640], half_m, c_m, sf_m, sb_m)
    q = _dot(cq, wuq_ref[...])
    kv = _dot(ckv, wukv_ref[...])
    lane = lax.broadcasted_iota(jnp.int32, (1, LANES), 1)
    ones_col = (lane == MLA_V_DIM).astype(F32)
    hw = MLA_HEADS * MLA_HEAD_PAD
    for hd in range(MLA_HEADS):
        blk = slice(hd * MLA_HEAD_PAD, (hd + 1) * MLA_HEAD_PAD)
        qm_ref[:, blk] = _rope_apply(q[:, blk], half_m, cq_m, sfq_m, sbq_m).astype(BF16)
        km_ref[:, blk] = (kv[:, blk] + kr).astype(BF16)
        vblk = slice(hw + hd * MLA_HEAD_PAD, hw + (hd + 1) * MLA_HEAD_PAD)
        vm_ref[:, blk] = (kv[:, vblk] + ones_col).astype(BF16)

    c_d, sf_d, sb_d = _rope_tables(pos, rc[3:4], rc[4:5], rc[5:6])
    d_scale = DIL_HEAD_DIM ** -0.5
    cq_d, sfq_d, sbq_d = c_d * d_scale, sf_d * d_scale, sb_d * d_scale
    half_d = DIL_ROPE_DIM // 2
    zd = _dot(h, w_ref[:, 640:640 + 3 * DIL_WIDTH])
    nblk = DIL_WIDTH // LANES
    for j in range(nblk):
        blk = slice(j * LANES, (j + 1) * LANES)
        zd_ref[j] = _rope_apply(zd[:, blk], half_d, cq_d, sfq_d, sbq_d)
        kblk = slice(DIL_WIDTH + j * LANES, DIL_WIDTH + (j + 1) * LANES)
        zd_ref[nblk + j] = _rope_apply(zd[:, kblk], half_d, c_d, sf_d, sb_d)
        vblk = slice(2 * DIL_WIDTH + j * LANES, 2 * DIL_WIDTH + (j + 1) * LANES)
        zd_ref[2 * nblk + j] = zd[:, vblk]

    outs = ((q0_ref, k0_ref, v0_ref), (q1_ref, k1_ref, v1_ref), (q2_ref, k2_ref, v2_ref))
    gblk = DIL_GROUP_WIDTH // LANES
    for g, (_, dil) in enumerate(DIL_GROUPS):
        for part in range(3):
            o_ref = outs[g][part]
            for jj in range(gblk):
                j = part * nblk + g * gblk + jj
                cols = slice(jj * LANES, (jj + 1) * LANES)
                if dil == 1:
                    o_ref[0, :, cols] = zd_ref[j].astype(BF16)
                else:
                    for r in range(dil):
                        o_ref[r, :, cols] = zd_ref[j, pl.ds(r, tm // dil, stride=dil), :].astype(BF16)

    mq = _dot(h, w_ref[:, 640 + 3 * DIL_WIDTH:])
    mq_ref[...] = (mq * (MEM_HEAD_DIM ** -0.5)).astype(BF16)


def _mixproj(x1, pos, g, w_b, qn, wuq, kvn, wukv, rope_consts, *, tm=512):
    b, s, d = x1.shape
    assert s % tm == 0 and tm % 16 == 0
    hw = MLA_HEADS * MLA_HEAD_PAD
    tok = lambda width: pl.BlockSpec((None, tm, width), lambda bi, i: (bi, i, 0))
    in_specs = [
        tok(d),
        tok(1),
        _const_spec(g.shape),
        _const_spec(w_b.shape),
        _const_spec(qn.shape),
        _const_spec(wuq.shape),
        _const_spec(kvn.shape),
        _const_spec(wukv.shape),
        _const_spec(rope_consts.shape),
    ]
    out_shapes = [jax.ShapeDtypeStruct((b, s, hw), BF16)] * 3
    out_specs = [tok(hw)] * 3
    for _, dil in DIL_GROUPS:
        for _ in range(3):
            out_shapes.append(jax.ShapeDtypeStruct((b, dil, s // dil, DIL_GROUP_WIDTH), BF16))
            out_specs.append(pl.BlockSpec((None, dil, tm // dil, DIL_GROUP_WIDTH),
                                          lambda bi, i: (bi, 0, i, 0)))
    out_shapes.append(jax.ShapeDtypeStruct((b, s, MEM_WIDTH), BF16))
    out_specs.append(tok(MEM_WIDTH))
    return pl.pallas_call(
        _mixproj_kernel,
        grid=(b, s // tm),
        in_specs=in_specs,
        out_specs=out_specs,
        out_shape=out_shapes,
        scratch_shapes=[pltpu.VMEM((3 * DIL_WIDTH // LANES, tm, LANES), F32)],
        compiler_params=_params(2),
        name="mixproj",
    )(x1, pos, g, w_b, qn, wuq, kvn, wukv, rope_consts)


def _mla_attn_kernel(q_ref, k_ref, v_ref, o_ref, *, tk):
    tq = q_ref.shape[0]
    s_len = k_ref.shape[0]
    heads = q_ref.shape[1] // MLA_HEAD_PAD
    outs = []
    for hd in range(heads):
        blk = slice(hd * MLA_HEAD_PAD, (hd + 1) * MLA_HEAD_PAD)
        q = q_ref[:, blk]

        def body(j, carry, blk=blk, q=q):
            m, acc = carry
            rows = pl.ds(pl.multiple_of(j * tk, tk), tk)
            s = _dot_nt(q, k_ref[rows, blk])
            m_new = jnp.maximum(m, jnp.max(s, axis=-1, keepdims=True))
            p = jnp.exp(s - m_new).astype(BF16)
            acc = jnp.exp(m - m_new) * acc + _dot(p, v_ref[rows, blk])
            return m_new, acc

        m0 = jnp.full((tq, 1), NEG_INF, F32)
        acc0 = jnp.zeros((tq, MLA_HEAD_PAD), F32)
        _, acc = lax.fori_loop(0, s_len // tk, body, (m0, acc0))
        outs.append(acc * (1.0 / acc[:, MLA_V_DIM:MLA_V_DIM + 1]))
    lane = lax.broadcasted_iota(jnp.int32, (1, LANES), 1)
    for pr in range(heads // 2):
        even, odd = outs[2 * pr], outs[2 * pr + 1]
        pair = jnp.where(lane < MLA_V_DIM, even, pltpu.roll(odd, MLA_V_DIM, axis=1))
        o_ref[:, pr * LANES:(pr + 1) * LANES] = pair.astype(BF16)


def _mla_attn(q, k, v, *, tq=512, tk=512, heads_per_step=2):
    b, s, hw = q.shape
    assert s % tq == 0 and s % tk == 0 and heads_per_step % 2 == 0
    wblk = heads_per_step * MLA_HEAD_PAD
    oblk = heads_per_step * MLA_V_DIM
    return pl.pallas_call(
        functools.partial(_mla_attn_kernel, tk=tk),
        grid=(b, hw // wblk, s // tq),
        in_specs=[
            pl.BlockSpec((None, tq, wblk), lambda bi, hi, i: (bi, i, hi)),
            pl.BlockSpec((None, s, wblk), lambda bi, hi, i: (bi, 0, hi)),
            pl.BlockSpec((None, s, wblk), lambda bi, hi, i: (bi, 0, hi)),
        ],
        out_specs=pl.BlockSpec((None, tq, oblk), lambda bi, hi, i: (bi, i, hi)),
        out_shape=jax.ShapeDtypeStruct((b, s, MLA_HEADS * MLA_V_DIM), BF16),
        compiler_params=_params(3),
        name="mla_attn",
    )(q, k, v)


def _dil_attn_kernel(q_ref, k_ref, v_ref, o_ref, lse_ref, *, n_side):
    tq_all = q_ref.shape[0]
    length = k_ref.shape[0]
    i = pl.program_id(2)
    lane = lax.broadcasted_iota(jnp.int32, (1, LANES), 1)
    for t in range(tq_all // DIL_Q_TILE):
        qs = i * tq_all + t * DIL_Q_TILE
        ks = pl.multiple_of(jnp.clip(qs - n_side, 0, length - DIL_K_TILE), n_side)
        kidx = ks + lax.broadcasted_iota(jnp.int32, (DIL_Q_TILE, DIL_K_TILE), 1)
        qidx = qs + lax.broadcasted_iota(jnp.int32, (DIL_Q_TILE, DIL_K_TILE), 0)
        valid = jnp.abs(kidx - qidx) <= n_side
        rows = slice(t * DIL_Q_TILE, (t + 1) * DIL_Q_TILE)
        for pr in range(DIL_GROUP_WIDTH // LANES):
            cols = slice(pr * LANES, (pr + 1) * LANES)
            qp = q_ref[rows, cols]
            kp = k_ref[pl.ds(ks, DIL_K_TILE), cols]
            vp = v_ref[pl.ds(ks, DIL_K_TILE), cols]
            o_pair = jnp.zeros((DIL_Q_TILE, LANES), F32)
            lse_pair = jnp.zeros((DIL_Q_TILE, LANES), F32)
            for hh in range(2):
                in_head = (lane >= DIL_HEAD_DIM) if hh else (lane < DIL_HEAD_DIM)
                k_h = jnp.where(in_head, kp, jnp.zeros_like(kp))
                v_h = jnp.where(in_head, vp, jnp.zeros_like(vp))
                s = jnp.where(valid, _dot_nt(qp, k_h), NEG_INF)
                m = jnp.max(s, axis=-1, keepdims=True)
                p = jnp.exp(s - m)
                l = jnp.sum(p, axis=-1, keepdims=True)
                o_pair = o_pair + _dot(p.astype(BF16), v_h) * (1.0 / l)
                lse_pair = jnp.where(in_head, m + jnp.log(l), lse_pair)
            o_ref[rows, cols] = o_pair.astype(BF16)
            lse_ref[rows, cols] = lse_pair


def _dil_attn(q, k, v, n_side, *, tq_max=512):
    b, dil, length, w = q.shape
    assert w == DIL_GROUP_WIDTH and n_side == DIL_Q_TILE // 2
    assert DIL_K_TILE == DIL_Q_TILE + 2 * n_side and length >= DIL_K_TILE
    tq = min(length, tq_max)
    assert length % tq == 0 and tq % DIL_Q_TILE == 0
    q_spec = pl.BlockSpec((None, None, tq, w), lambda bi, r, i: (bi, r, i, 0))
    kv_spec = pl.BlockSpec((None, None, length, w), lambda bi, r, i: (bi, r, 0, 0))
    return pl.pallas_call(
        functools.partial(_dil_attn_kernel, n_side=n_side),
        grid=(b, dil, length // tq),
        in_specs=[q_spec, kv_spec, kv_spec],
        out_specs=[q_spec, q_spec],
        out_shape=[jax.ShapeDtypeStruct(q.shape, BF16), jax.ShapeDtypeStruct(q.shape, F32)],
        compiler_params=_params(3),
        name=f"dil_attn_d{dil}",
    )(q, k, v)


def _mem_kv_kernel(mem_ref, g_ref, w_ref, k_ref, v_ref):
    hm = _rms(mem_ref[...], g_ref[...]).astype(BF16)
    kv = _dot(hm, w_ref[...])
    k_ref[...] = kv[:, :MEM_WIDTH].astype(BF16)
    v_ref[...] = kv[:, MEM_WIDTH:].astype(BF16)


def _mem_kv(mem, g, w_kv):
    b, m, d = mem.shape
    spec_out = pl.BlockSpec((None, m, MEM_WIDTH), lambda bi: (bi, 0, 0))
    return pl.pallas_call(
        _mem_kv_kernel,
        grid=(b,),
        in_specs=[pl.BlockSpec((None, m, d), lambda bi: (bi, 0, 0)),
                  _const_spec(g.shape), _const_spec(w_kv.shape)],
        out_specs=[spec_out, spec_out],
        out_shape=[jax.ShapeDtypeStruct((b, m, MEM_WIDTH), BF16)] * 2,
        compiler_params=_params(1),
        name="mem_kv",
    )(mem, g, w_kv)


def _mix_kernel(x_ref, g_ref, wgate_ref, omla_ref, womla_ref,
                o0_ref, l0_ref, o1_ref, l1_ref, o2_ref, l2_ref, wodil_ref,
                mq_ref, mk_ref, mv_ref, womem_ref, wout_ref,
                out_ref, so1_ref, sl1_ref, so2_ref, sl2_ref):
    x = x_ref[...]
    d = x.shape[1]
    h = _rms(x, g_ref[...]).astype(BF16)

    o_mem = []
    for hd in range(MEM_HEADS):
        blk = slice(hd * MEM_HEAD_DIM, (hd + 1) * MEM_HEAD_DIM)
        s = _dot_nt(mq_ref[:, blk], mk_ref[:, blk])
        m = jnp.max(s, axis=-1, keepdims=True)
        p = jnp.exp(s - m)
        l = jnp.sum(p, axis=-1, keepdims=True)
        o_mem.append((_dot(p.astype(BF16), mv_ref[:, blk]) * (1.0 / l)).astype(BF16))
    y_mem = _dot(jnp.concatenate(o_mem, axis=1), womem_ref[...])

    gblk = DIL_GROUP_WIDTH // LANES
    for o_ref, l_ref, so_ref, sl_ref in ((o1_ref, l1_ref, so1_ref, sl1_ref),
                                         (o2_ref, l2_ref, so2_ref, sl2_ref)):
        dil, n = o_ref.shape[0], o_ref.shape[1]
        for r in range(dil):
            rows = pl.ds(r, n, stride=dil)
            for jj in range(gblk):
                cols = slice(jj * LANES, (jj + 1) * LANES)
                so_ref[jj, rows, :] = o_ref[r, :, cols].astype(F32)
                sl_ref[jj, rows, :] = l_ref[r, :, cols]
    o_parts = []
    for jj in range(gblk):
        cols = slice(jj * LANES, (jj + 1) * LANES)
        l0, l1, l2 = l0_ref[:, cols], sl1_ref[jj], sl2_ref[jj]
        lmax = jnp.maximum(jnp.maximum(l0, l1), l2)
        w0, w1, w2 = jnp.exp(l0 - lmax), jnp.exp(l1 - lmax), jnp.exp(l2 - lmax)
        o_sum = w0 * o0_ref[:, cols].astype(F32) + w1 * so1_ref[jj] + w2 * so2_ref[jj]
        o_parts.append((o_sum * (1.0 / (w0 + w1 + w2))).astype(BF16))
    y_dil = _dot(jnp.concatenate(o_parts, axis=1), wodil_ref[...])

    y_mla = _dot(omla_ref[...], womla_ref[...])

    mixed = jax.nn.sigmoid(_dot(h, wgate_ref[:, 0:d])) * y_mla
    mixed = mixed + jax.nn.sigmoid(_dot(h, wgate_ref[:, d:2 * d])) * y_dil
    mixed = mixed + jax.nn.sigmoid(_dot(h, wgate_ref[:, 2 * d:3 * d])) * y_mem
    out_ref[...] = x + _dot(mixed.astype(BF16), wout_ref[...])


def _mix(x1, g, w_gate, o_mla, wo_mla, dil_outs, wo_dil, mq, mem_k, mem_v, wo_mem, w_out, *, tm=512):
    b, s, d = x1.shape
    assert s % tm == 0
    n_mem = mem_k.shape[1]
    tok = lambda width: pl.BlockSpec((None, tm, width), lambda bi, i: (bi, i, 0))
    in_specs = [tok(d), _const_spec(g.shape), _const_spec(w_gate.shape),
                tok(o_mla.shape[2]), _const_spec(wo_mla.shape)]
    args = [x1, g, w_gate, o_mla, wo_mla]
    for (o_g, lse_g), (_, dil) in zip(dil_outs, DIL_GROUPS):
        if dil == 1:
            spec = tok(DIL_GROUP_WIDTH)
            o_g, lse_g = o_g.reshape(b, s, DIL_GROUP_WIDTH), lse_g.reshape(b, s, DIL_GROUP_WIDTH)
        else:
            spec = pl.BlockSpec((None, dil, tm // dil, DIL_GROUP_WIDTH), lambda bi, i: (bi, 0, i, 0))
        in_specs += [spec, spec]
        args += [o_g, lse_g]
    in_specs += [_const_spec(wo_dil.shape), tok(MEM_WIDTH),
                 pl.BlockSpec((None, n_mem, MEM_WIDTH), lambda bi, i: (bi, 0, 0)),
                 pl.BlockSpec((None, n_mem, MEM_WIDTH), lambda bi, i: (bi, 0, 0)),
                 _const_spec(wo_mem.shape), _const_spec(w_out.shape)]
    args += [wo_dil, mq, mem_k, mem_v, wo_mem, w_out]
    return pl.pallas_call(
        _mix_kernel,
        grid=(b, s // tm),
        in_specs=in_specs,
        out_specs=tok(d),
        out_shape=jax.ShapeDtypeStruct((b, s, d), F32),
        scratch_shapes=[pltpu.VMEM((DIL_GROUP_WIDTH // LANES, tm, LANES), F32)] * 4,
        compiler_params=_params(2),
        name="mix",
    )(*args)


def _rope_consts():
    rows = jnp.zeros((8, LANES), F32)
    half_m = MLA_ROPE_DIM // 2
    f_m = ROPE_THETA ** (-2.0 * jnp.arange(half_m, dtype=F32) / MLA_ROPE_DIM)
    lo, mid, hi = MLA_NOPE_DIM, MLA_NOPE_DIM + half_m, MLA_NOPE_DIM + 2 * half_m
    rows = rows.at[0, lo:mid].set(f_m).at[0, mid:hi].set(f_m)
    rows = rows.at[1, mid:hi].set(1.0)
    rows = rows.at[2, lo:mid].set(-1.0)
    half_d = DIL_ROPE_DIM // 2
    f_d = ROPE_THETA ** (-2.0 * jnp.arange(half_d, dtype=F32) / DIL_ROPE_DIM)
    for base in range(0, LANES, DIL_HEAD_DIM):
        rows = rows.at[3, base:base + half_d].set(f_d).at[3, base + half_d:base + 2 * half_d].set(f_d)
        rows = rows.at[4, base + half_d:base + 2 * half_d].set(1.0)
        rows = rows.at[5, base:base + half_d].set(-1.0)
    return rows


def _prep_mix_weights(w_in, w_uq, w_ukv):
    d = w_in.shape[0]
    n_lat = MLA_Q_RANK + MLA_KV_RANK
    kr0 = n_lat
    dil0 = kr0 + MLA_ROPE_DIM
    mq0 = dil0 + 3 * DIL_WIDTH
    gate0 = mq0 + MEM_WIDTH
    kr_block = jnp.concatenate([
        jnp.zeros((d, MLA_NOPE_DIM), F32), w_in[:, kr0:dil0],
        jnp.zeros((d, MLA_HEAD_PAD - MLA_QK_DIM), F32)], axis=1)
    w_b = jnp.concatenate([w_in[:, :n_lat], kr_block, w_in[:, dil0:gate0]], axis=1).astype(BF16)
    w_gate = w_in[:, gate0:].astype(BF16)
    wuq = jnp.pad(w_uq.reshape(MLA_Q_RANK, MLA_HEADS, MLA_QK_DIM),
                  ((0, 0), (0, 0), (0, MLA_HEAD_PAD - MLA_QK_DIM)))
    wuq = wuq.reshape(MLA_Q_RANK, MLA_HEADS * MLA_HEAD_PAD).astype(BF16)
    wkv = w_ukv.reshape(MLA_KV_RANK, MLA_HEADS, MLA_NOPE_DIM + MLA_V_DIM)
    pad = ((0, 0), (0, 0), (0, MLA_HEAD_PAD - MLA_NOPE_DIM))
    wk = jnp.pad(wkv[..., :MLA_NOPE_DIM], pad).reshape(MLA_KV_RANK, -1)
    wv = jnp.pad(wkv[..., MLA_NOPE_DIM:], pad).reshape(MLA_KV_RANK, -1)
    wukv = jnp.concatenate([wk, wv], axis=1).astype(BF16)
    return w_b, w_gate, wuq, wukv


def kernel(x, mem, positions, ffn1_norm, ffn1_w_gate, ffn1_w_up, ffn1_w_down, mix_norm, w_in, mla_q_norm, mla_w_uq, mla_kv_norm, mla_w_ukv, mla_w_o, dil_w_o, mem_norm, mem_w_kv, mem_w_o, w_out, ffn2_norm, ffn2_w_gate, ffn2_w_up, ffn2_w_down, final_norm):
    b, s, d = x.shape
    depth = ffn1_norm.shape[0]
    pos = positions.astype(F32).reshape(b, s, 1)
    rope_consts = _rope_consts()
    row = lambda v: v.reshape(1, -1)
    bf = lambda w: w.astype(BF16)
    for l in range(depth):
        x = _ffn(x.reshape(b * s, d), row(ffn1_norm[l]), bf(ffn1_w_gate[l]), bf(ffn1_w_up[l]),
                 bf(ffn1_w_down[l])).reshape(b, s, d)

        w_b, w_gate, wuq, wukv = _prep_mix_weights(w_in[l], mla_w_uq[l], mla_w_ukv[l])
        outs = _mixproj(x, pos, row(mix_norm[l]), w_b, row(mla_q_norm[l]), wuq,
                        row(mla_kv_norm[l]), wukv, rope_consts)
        q_m, k_m, v_m = outs[0:3]
        mq = outs[12]
        o_mla = _mla_attn(q_m, k_m, v_m)
        dil_outs = []
        for g, (window, dil) in enumerate(DIL_GROUPS):
            qg, kg, vg = outs[3 + 3 * g:6 + 3 * g]
            dil_outs.append(_dil_attn(qg, kg, vg, window // (2 * dil)))
        mem_k, mem_v = _mem_kv(mem, row(mem_norm[l]), bf(mem_w_kv[l]))
        x = _mix(x, row(mix_norm[l]), w_gate, o_mla, bf(mla_w_o[l]), dil_outs, bf(dil_w_o[l]),
                 mq, mem_k, mem_v, bf(mem_w_o[l]), bf(w_out[l]))

        last = l == depth - 1
        x = _ffn(x.reshape(b * s, d), row(ffn2_norm[l]), bf(ffn2_w_gate[l]), bf(ffn2_w_up[l]),
                 bf(ffn2_w_down[l]), row(final_norm) if last else None).reshape(b, s, d)
    return x
```

```python
import functools

import jax
import jax.numpy as jnp
from jax import lax
from jax.experimental import pallas as pl
from jax.experimental.pallas import tpu as pltpu

F32 = jnp.float32
BF16 = jnp.bfloat16

NORM_EPS = 1e-6
NEG_INF = -1e30
ROPE_THETA = 500000.0
LOG2_E = 1.4426950408889634

LANES = 128
VMEM_LIMIT_BYTES = 56 * 1024 * 1024

MLA_HEADS = 8
MLA_Q_RANK = 384
MLA_KV_RANK = 128
MLA_NOPE_DIM = 64
MLA_ROPE_DIM = 32
MLA_V_DIM = 64
MLA_QK_DIM = MLA_NOPE_DIM + MLA_ROPE_DIM
MLA_HEAD_PAD = 128

DIL_GROUPS = ((128, 1), (512, 4), (2048, 16))
DIL_HEADS = 4
DIL_HEAD_DIM = 64
DIL_ROPE_DIM = 16
DIL_GROUP_WIDTH = DIL_HEADS * DIL_HEAD_DIM
DIL_WIDTH = len(DIL_GROUPS) * DIL_GROUP_WIDTH
DIL_Q_TILE = 128
DIL_K_TILE = 256

MEM_HEADS = 4
MEM_HEAD_DIM = 128
MEM_WIDTH = MEM_HEADS * MEM_HEAD_DIM

N_BRANCHES = 3


def _dot(a, b):
    return jnp.dot(a, b, preferred_element_type=F32)


def _dot_nt(a, b):
    return lax.dot_general(a, b, (((1,), (1,)), ((), ())), preferred_element_type=F32)


def _rms(x, g):
    y = x * lax.rsqrt(jnp.mean(x * x, axis=-1, keepdims=True) + NORM_EPS)
    return y * g


def _const_spec(shape):
    nd = len(shape)
    return pl.BlockSpec(shape, lambda *_: (0,) * nd, pipeline_mode=pl.Buffered(1))


def _params(n_grid):
    return pltpu.CompilerParams(
        dimension_semantics=("arbitrary",) * n_grid,
        vmem_limit_bytes=VMEM_LIMIT_BYTES,
    )


def _ffn_kernel(x_ref, g_ref, wg_ref, wu_ref, wd_ref, *rest, f_chunk, final):
    if final:
        fg_ref, o_ref = rest
    else:
        (o_ref,) = rest
    x = x_ref[...]
    h = _rms(x, g_ref[...]).astype(BF16)
    d_ff = wg_ref.shape[1]
    acc = jnp.zeros(x.shape, F32)
    for c in range(d_ff // f_chunk):
        cols = slice(c * f_chunk, (c + 1) * f_chunk)
        a = _dot(h, wg_ref[:, cols])
        b = _dot(h, wu_ref[:, cols])
        act = (jax.nn.silu(a) * b).astype(BF16)
        acc = acc + _dot(act, wd_ref[cols, :])
    y = x + 0.5 * acc
    if final:
        y = _rms(y, fg_ref[...])
    o_ref[...] = y


def _ffn(x2d, g, wg, wu, wd, final_g=None, *, tm=512, f_chunk=256):
    t, d = x2d.shape
    d_ff = wg.shape[1]
    assert t % tm == 0 and d_ff % f_chunk == 0
    final = final_g is not None
    in_specs = [
        pl.BlockSpec((tm, d), lambda i: (i, 0)),
        _const_spec((1, d)),
        _const_spec((d, d_ff)),
        _const_spec((d, d_ff)),
        _const_spec((d_ff, d)),
    ]
    args = [x2d, g, wg, wu, wd]
    if final:
        in_specs.append(_const_spec((1, d)))
        args.append(final_g)
    return pl.pallas_call(
        functools.partial(_ffn_kernel, f_chunk=f_chunk, final=final),
        grid=(t // tm,),
        in_specs=in_specs,
        out_specs=pl.BlockSpec((tm, d), lambda i: (i, 0)),
        out_shape=jax.ShapeDtypeStruct((t, d), F32),
        compiler_params=_params(1),
        name="ffn_final" if final else "ffn",
    )(*args)


def _rope_tables(pos, invf, m_fwd, m_bwd):
    ang = pos * invf
    c = jnp.cos(ang)
    s = jnp.sin(ang)
    return c, s * m_fwd, s * m_bwd


def _rope_apply(x, half, c, s_fwd, s_bwd):
    return (x * c
            + pltpu.roll(x, half, axis=1) * s_fwd
            + pltpu.roll(x, LANES - half, axis=1) * s_bwd)


def _mixproj_kernel(x_ref, pos_ref, g_ref, w_ref, qn_ref, wuq_ref, kvn_ref, wukv_ref, rc_ref,
                    qm_ref, km_ref, vm_ref,
                    q0_ref, k0_ref, v0_ref, q1_ref, k1_ref, v1_ref, q2_ref, k2_ref, v2_ref,
                    mq_ref, zd_ref):
    tm = x_ref.shape[0]
    h = _rms(x_ref[...], g_ref[...]).astype(BF16)
    pos = pos_ref[...]
    rc = rc_ref[...]

    c_m, sf_m, sb_m = _rope_tables(pos, rc[0:1], rc[1:2], rc[2:3])
    q_scale = MLA_QK_DIM ** -0.5 * LOG2_E
    cq_m, sfq_m, sbq_m = c_m * q_scale, sf_m * q_scale, sb_m * q_scale
    half_m = MLA_ROPE_DIM // 2

    z = _dot(h, w_ref[:, 0:640])
    cq = _rms(z[:, 0:MLA_Q_RANK], qn_ref[...]).astype(BF16)
    ckv = _rms(z[:, MLA_Q_RANK:MLA_Q_RANK + MLA_KV_RANK], kvn_ref[...]).astype(BF16)
    kr = _rope_apply(z[:, 512:640], half_m, c_m, sf_m, sb_m)
    q = _dot(cq, wuq_ref[...])
    kv = _dot(ckv, wukv_ref[...])
    lane = lax.broadcasted_iota(jnp.int32, (1, LANES), 1)
    ones_col = (lane == MLA_V_DIM).astype(F32)
    hw = MLA_HEADS * MLA_HEAD_PAD
    for hd in range(MLA_HEADS):
        blk = slice(hd * MLA_HEAD_PAD, (hd + 1) * MLA_HEAD_PAD)
        qm_ref[:, blk] = _rope_apply(q[:, blk], half_m, cq_m, sfq_m, sbq_m).astype(BF16)
        km_ref[:, blk] = (kv[:, blk] + kr).astype(BF16)
        vblk = slice(hw + hd * MLA_HEAD_PAD, hw + (hd + 1) * MLA_HEAD_PAD)
        vm_ref[:, blk] = (kv[:, vblk] + ones_col).astype(BF16)

    c_d, sf_d, sb_d = _rope_tables(pos, rc[3:4], rc[4:5], rc[5:6])
    d_scale = DIL_HEAD_DIM ** -0.5
    cq_d, sfq_d, sbq_d = c_d * d_scale, sf_d * d_scale, sb_d * d_scale
    half_d = DIL_ROPE_DIM // 2
    zd = _dot(h, w_ref[:, 640:640 + 3 * DIL_WIDTH])
    nblk = DIL_WIDTH // LANES
    for j in range(nblk):
        blk = slice(j * LANES, (j + 1) * LANES)
        zd_ref[j] = _rope_apply(zd[:, blk], half_d, cq_d, sfq_d, sbq_d)
        kblk = slice(DIL_WIDTH + j * LANES, DIL_WIDTH + (j + 1) * LANES)
        zd_ref[nblk + j] = _rope_apply(zd[:, kblk], half_d, c_d, sf_d, sb_d)
        vblk = slice(2 * DIL_WIDTH + j * LANES, 2 * DIL_WIDTH + (j + 1) * LANES)
        zd_ref[2 * nblk + j] = zd[:, vblk]

    outs = ((q0_ref, k0_ref, v0_ref), (q1_ref, k1_ref, v1_ref), (q2_ref, k2_ref, v2_ref))
    gblk = DIL_GROUP_WIDTH // LANES
    for g, (_, dil) in enumerate(DIL_GROUPS):
        for part in range(3):
            o_ref = outs[g][part]
            for jj in range(gblk):
                j = part * nblk + g * gblk + jj
                cols = slice(jj * LANES, (jj + 1) * LANES)
                if dil == 1:
                    o_ref[0, :, cols] = zd_ref[j].astype(BF16)
                else:
                    for r in range(dil):
                        o_ref[r, :, cols] = zd_ref[j, pl.ds(r, tm // dil, stride=dil), :].astype(BF16)

    mq = _dot(h, w_ref[:, 640 + 3 * DIL_WIDTH:])
    mq_ref[...] = (mq * (MEM_HEAD_DIM ** -0.5)).astype(BF16)


def _mixproj(x1, pos, g, w_b, qn, wuq, kvn, wukv, rope_consts, *, tm=512):
    b, s, d = x1.shape
    assert s % tm == 0 and tm % 16 == 0
    hw = MLA_HEADS * MLA_HEAD_PAD
    tok = lambda width: pl.BlockSpec((None, tm, width), lambda bi, i: (bi, i, 0))
    in_specs = [
        tok(d),
        tok(1),
        _const_spec(g.shape),
        _const_spec(w_b.shape),
        _const_spec(qn.shape),
        _const_spec(wuq.shape),
        _const_spec(kvn.shape),
        _const_spec(wukv.shape),
        _const_spec(rope_consts.shape),
    ]
    out_shapes = [jax.ShapeDtypeStruct((b, s, hw), BF16)] * 3
    out_specs = [tok(hw)] * 3
    for _, dil in DIL_GROUPS:
        for _ in range(3):
            out_shapes.append(jax.ShapeDtypeStruct((b, dil, s // dil, DIL_GROUP_WIDTH), BF16))
            out_specs.append(pl.BlockSpec((None, dil, tm // dil, DIL_GROUP_WIDTH),
                                          lambda bi, i: (bi, 0, i, 0)))
    out_shapes.append(jax.ShapeDtypeStruct((b, s, MEM_WIDTH), BF16))
    out_specs.append(tok(MEM_WIDTH))
    return pl.pallas_call(
        _mixproj_kernel,
        grid=(b, s // tm),
        in_specs=in_specs,
        out_specs=out_specs,
        out_shape=out_shapes,
        scratch_shapes=[pltpu.VMEM((3 * DIL_WIDTH // LANES, tm, LANES), F32)],
        compiler_params=_params(2),
        name="mixproj",
    )(x1, pos, g, w_b, qn, wuq, kvn, wukv, rope_consts)


def _mla_attn_kernel(q_ref, k_ref, v_ref, o_ref, *, tk, unroll):
    tq = q_ref.shape[0]
    s_len = k_ref.shape[0]
    heads = q_ref.shape[1] // MLA_HEAD_PAD
    blks = [slice(hd * MLA_HEAD_PAD, (hd + 1) * MLA_HEAD_PAD) for hd in range(heads)]

    def body(j, carry):
        rows = pl.ds(pl.multiple_of(j * tk, tk), tk)
        new = []
        for blk, (m, acc) in zip(blks, carry):
            s = _dot_nt(q_ref[:, blk], k_ref[rows, blk])
            m_new = jnp.maximum(m, jnp.max(s, axis=-1, keepdims=True))
            p = jnp.exp2(s - m_new).astype(BF16)
            acc = jnp.exp2(m - m_new) * acc + _dot(p, v_ref[rows, blk])
            new.append((m_new, acc))
        return tuple(new)

    init = tuple((jnp.full((tq, 1), NEG_INF, F32), jnp.zeros((tq, MLA_HEAD_PAD), F32))
                 for _ in range(heads))
    final = lax.fori_loop(0, s_len // tk, body, init, unroll=unroll)
    outs = [acc * (1.0 / acc[:, MLA_V_DIM:MLA_V_DIM + 1]) for _, acc in final]
    lane = lax.broadcasted_iota(jnp.int32, (1, LANES), 1)
    for pr in range(heads // 2):
        even, odd = outs[2 * pr], outs[2 * pr + 1]
        pair = jnp.where(lane < MLA_V_DIM, even, pltpu.roll(odd, MLA_V_DIM, axis=1))
        o_ref[:, pr * LANES:(pr + 1) * LANES] = pair.astype(BF16)


def _mla_attn(q, k, v, *, tq=1024, tk=2048, heads_per_step=2, unroll=1):
    b, s, hw = q.shape
    assert s % tq == 0 and s % tk == 0 and heads_per_step % 2 == 0
    wblk = heads_per_step * MLA_HEAD_PAD
    oblk = heads_per_step * MLA_V_DIM
    return pl.pallas_call(
        functools.partial(_mla_attn_kernel, tk=tk, unroll=unroll),
        grid=(b, hw // wblk, s // tq),
        in_specs=[
            pl.BlockSpec((None, tq, wblk), lambda bi, hi, i: (bi, i, hi)),
            pl.BlockSpec((None, s, wblk), lambda bi, hi, i: (bi, 0, hi)),
            pl.BlockSpec((None, s, wblk), lambda bi, hi, i: (bi, 0, hi)),
        ],
        out_specs=pl.BlockSpec((None, tq, oblk), lambda bi, hi, i: (bi, i, hi)),
        out_shape=jax.ShapeDtypeStruct((b, s, MLA_HEADS * MLA_V_DIM), BF16),
        compiler_params=_params(3),
        name="mla_attn",
    )(q, k, v)


def _dil_attn_kernel(q_ref, k_ref, v_ref, o_ref, lse_ref, *, n_side):
    tq_all = q_ref.shape[0]
    length = k_ref.shape[0]
    i = pl.program_id(2)
    lane = lax.broadcasted_iota(jnp.int32, (1, LANES), 1)
    for t in range(tq_all // DIL_Q_TILE):
        qs = i * tq_all + t * DIL_Q_TILE
        ks = pl.multiple_of(jnp.clip(qs - n_side, 0, length - DIL_K_TILE), n_side)
        kidx = ks + lax.broadcasted_iota(jnp.int32, (DIL_Q_TILE, DIL_K_TILE), 1)
        qidx = qs + lax.broadcasted_iota(jnp.int32, (DIL_Q_TILE, DIL_K_TILE), 0)
        valid = jnp.abs(kidx - qidx) <= n_side
        rows = slice(t * DIL_Q_TILE, (t + 1) * DIL_Q_TILE)
        for pr in range(DIL_GROUP_WIDTH // LANES):
            cols = slice(pr * LANES, (pr + 1) * LANES)
            qp = q_ref[rows, cols]
            kp = k_ref[pl.ds(ks, DIL_K_TILE), cols]
            vp = v_ref[pl.ds(ks, DIL_K_TILE), cols]
            o_pair = jnp.zeros((DIL_Q_TILE, LANES), F32)
            lse_pair = jnp.zeros((DIL_Q_TILE, LANES), F32)
            for hh in range(2):
                in_head = (lane >= DIL_HEAD_DIM) if hh else (lane < DIL_HEAD_DIM)
                k_h = jnp.where(in_head, kp, jnp.zeros_like(kp))
                v_h = jnp.where(in_head, vp, jnp.zeros_like(vp))
                s = jnp.where(valid, _dot_nt(qp, k_h), NEG_INF)
                m = jnp.max(s, axis=-1, keepdims=True)
                p = jnp.exp(s - m)
                l = jnp.sum(p, axis=-1, keepdims=True)
                o_pair = o_pair + _dot(p.astype(BF16), v_h) * (1.0 / l)
                lse_pair = jnp.where(in_head, m + jnp.log(l), lse_pair)
            o_ref[rows, cols] = o_pair.astype(BF16)
            lse_ref[rows, cols] = lse_pair


def _dil_attn(q, k, v, n_side, *, tq_max=512):
    b, dil, length, w = q.shape
    assert w == DIL_GROUP_WIDTH and n_side == DIL_Q_TILE // 2
    assert DIL_K_TILE == DIL_Q_TILE + 2 * n_side and length >= DIL_K_TILE
    tq = min(length, tq_max)
    assert length % tq == 0 and tq % DIL_Q_TILE == 0
    q_spec = pl.BlockSpec((None, None, tq, w), lambda bi, r, i: (bi, r, i, 0))
    kv_spec = pl.BlockSpec((None, None, length, w), lambda bi, r, i: (bi, r, 0, 0))
    return pl.pallas_call(
        functools.partial(_dil_attn_kernel, n_side=n_side),
        grid=(b, dil, length // tq),
        in_specs=[q_spec, kv_spec, kv_spec],
        out_specs=[q_spec, q_spec],
        out_shape=[jax.ShapeDtypeStruct(q.shape, BF16), jax.ShapeDtypeStruct(q.shape, F32)],
        compiler_params=_params(3),
        name=f"dil_attn_d{dil}",
    )(q, k, v)


def _mem_kv_kernel(mem_ref, g_ref, w_ref, k_ref, v_ref):
    hm = _rms(mem_ref[...], g_ref[...]).astype(BF16)
    kv = _dot(hm, w_ref[...])
    k_ref[...] = kv[:, :MEM_WIDTH].astype(BF16)
    v_ref[...] = kv[:, MEM_WIDTH:].astype(BF16)


def _mem_kv(mem, g, w_kv):
    b, m, d = mem.shape
    spec_out = pl.BlockSpec((None, m, MEM_WIDTH), lambda bi: (bi, 0, 0))
    return pl.pallas_call(
        _mem_kv_kernel,
        grid=(b,),
        in_specs=[pl.BlockSpec((None, m, d), lambda bi: (bi, 0, 0)),
                  _const_spec(g.shape), _const_spec(w_kv.shape)],
        out_specs=[spec_out, spec_out],
        out_shape=[jax.ShapeDtypeStruct((b, m, MEM_WIDTH), BF16)] * 2,
        compiler_params=_params(1),
        name="mem_kv",
    )(mem, g, w_kv)


def _mix_kernel(x_ref, g_ref, wgate_ref, omla_ref, womla_ref,
                o0_ref, l0_ref, o1_ref, l1_ref, o2_ref, l2_ref, wodil_ref,
                mq_ref, mk_ref, mv_ref, womem_ref, wout_ref,
                out_ref, so1_ref, sl1_ref, so2_ref, sl2_ref):
    x = x_ref[...]
    d = x.shape[1]
    h = _rms(x, g_ref[...]).astype(BF16)

    o_mem = []
    for hd in range(MEM_HEADS):
        blk = slice(hd * MEM_HEAD_DIM, (hd + 1) * MEM_HEAD_DIM)
        s = _dot_nt(mq_ref[:, blk], mk_ref[:, blk])
        m = jnp.max(s, axis=-1, keepdims=True)
        p = jnp.exp(s - m)
        l = jnp.sum(p, axis=-1, keepdims=True)
        o_mem.append((_dot(p.astype(BF16), mv_ref[:, blk]) * (1.0 / l)).astype(BF16))
    y_mem = _dot(jnp.concatenate(o_mem, axis=1), womem_ref[...])

    gblk = DIL_GROUP_WIDTH // LANES
    for o_ref, l_ref, so_ref, sl_ref in ((o1_ref, l1_ref, so1_ref, sl1_ref),
                                         (o2_ref, l2_ref, so2_ref, sl2_ref)):
        dil, n = o_ref.shape[0], o_ref.shape[1]
        for r in range(dil):
            rows = pl.ds(r, n, stride=dil)
            for jj in range(gblk):
                cols = slice(jj * LANES, (jj + 1) * LANES)
                so_ref[jj, rows, :] = o_ref[r, :, cols].astype(F32)
                sl_ref[jj, rows, :] = l_ref[r, :, cols]
    o_parts = []
    for jj in range(gblk):
        cols = slice(jj * LANES, (jj + 1) * LANES)
        l0, l1, l2 = l0_ref[:, cols], sl1_ref[jj], sl2_ref[jj]
        lmax = jnp.maximum(jnp.maximum(l0, l1), l2)
        w0, w1, w2 = jnp.exp(l0 - lmax), jnp.exp(l1 - lmax), jnp.exp(l2 - lmax)
        o_sum = w0 * o0_ref[:, cols].astype(F32) + w1 * so1_ref[jj] + w2 * so2_ref[jj]
        o_parts.append((o_sum * (1.0 / (w0 + w1 + w2))).astype(BF16))
    y_dil = _dot(jnp.concatenate(o_parts, axis=1), wodil_ref[...])

    y_mla = _dot(omla_ref[...], womla_ref[...])

    mixed = jax.nn.sigmoid(_dot(h, wgate_ref[:, 0:d])) * y_mla
    mixed = mixed + jax.nn.sigmoid(_dot(h, wgate_ref[:, d:2 * d])) * y_dil
    mixed = mixed + jax.nn.sigmoid(_dot(h, wgate_ref[:, 2 * d:3 * d])) * y_mem
    out_ref[...] = x + _dot(mixed.astype(BF16), wout_ref[...])


def _mix(x1, g, w_gate, o_mla, wo_mla, dil_outs, wo_dil, mq, mem_k, mem_v, wo_mem, w_out, *, tm=512):
    b, s, d = x1.shape
    assert s % tm == 0
    n_mem = mem_k.shape[1]
    tok = lambda width: pl.BlockSpec((None, tm, width), lambda bi, i: (bi, i, 0))
    in_specs = [tok(d), _const_spec(g.shape), _const_spec(w_gate.shape),
                tok(o_mla.shape[2]), _const_spec(wo_mla.shape)]
    args = [x1, g, w_gate, o_mla, wo_mla]
    for (o_g, lse_g), (_, dil) in zip(dil_outs, DIL_GROUPS):
        if dil == 1:
            spec = tok(DIL_GROUP_WIDTH)
            o_g, lse_g = o_g.reshape(b, s, DIL_GROUP_WIDTH), lse_g.reshape(b, s, DIL_GROUP_WIDTH)
        else:
            spec = pl.BlockSpec((None, dil, tm // dil, DIL_GROUP_WIDTH), lambda bi, i: (bi, 0, i, 0))
        in_specs += [spec, spec]
        args += [o_g, lse_g]
    in_specs += [_const_spec(wo_dil.shape), tok(MEM_WIDTH),
                 pl.BlockSpec((None, n_mem, MEM_WIDTH), lambda bi, i: (bi, 0, 0)),
                 pl.BlockSpec((None, n_mem, MEM_WIDTH), lambda bi, i: (bi, 0, 0)),
                 _const_spec(wo_mem.shape), _const_spec(w_out.shape)]
    args += [wo_dil, mq, mem_k, mem_v, wo_mem, w_out]
    return pl.pallas_call(
        _mix_kernel,
        grid=(b, s // tm),
        in_specs=in_specs,
        out_specs=tok(d),
        out_shape=jax.ShapeDtypeStruct((b, s, d), F32),
        scratch_shapes=[pltpu.VMEM((DIL_GROUP_WIDTH // LANES, tm, LANES), F32)] * 4,
        compiler_params=_params(2),
        name="mix",
    )(*args)


def _rope_consts():
    rows = jnp.zeros((8, LANES), F32)
    half_m = MLA_ROPE_DIM // 2
    f_m = ROPE_THETA ** (-2.0 * jnp.arange(half_m, dtype=F32) / MLA_ROPE_DIM)
    lo, mid, hi = MLA_NOPE_DIM, MLA_NOPE_DIM + half_m, MLA_NOPE_DIM + 2 * half_m
    rows = rows.at[0, lo:mid].set(f_m).at[0, mid:hi].set(f_m)
    rows = rows.at[1, mid:hi].set(1.0)
    rows = rows.at[2, lo:mid].set(-1.0)
    half_d = DIL_ROPE_DIM // 2
    f_d = ROPE_THETA ** (-2.0 * jnp.arange(half_d, dtype=F32) / DIL_ROPE_DIM)
    for base in range(0, LANES, DIL_HEAD_DIM):
        rows = rows.at[3, base:base + half_d].set(f_d).at[3, base + half_d:base + 2 * half_d].set(f_d)
        rows = rows.at[4, base + half_d:base + 2 * half_d].set(1.0)
        rows = rows.at[5, base:base + half_d].set(-1.0)
    return rows


def _prep_mix_weights(w_in, w_uq, w_ukv):
    d = w_in.shape[0]
    n_lat = MLA_Q_RANK + MLA_KV_RANK
    kr0 = n_lat
    dil0 = kr0 + MLA_ROPE_DIM
    mq0 = dil0 + 3 * DIL_WIDTH
    gate0 = mq0 + MEM_WIDTH
    kr_block = jnp.concatenate([
        jnp.zeros((d, MLA_NOPE_DIM), F32), w_in[:, kr0:dil0],
        jnp.zeros((d, MLA_HEAD_PAD - MLA_QK_DIM), F32)], axis=1)
    w_b = jnp.concatenate([w_in[:, :n_lat], kr_block, w_in[:, dil0:gate0]], axis=1).astype(BF16)
    w_gate = w_in[:, gate0:].astype(BF16)
    wuq = jnp.pad(w_uq.reshape(MLA_Q_RANK, MLA_HEADS, MLA_QK_DIM),
                  ((0, 0), (0, 0), (0, MLA_HEAD_PAD - MLA_QK_DIM)))
    wuq = wuq.reshape(MLA_Q_RANK, MLA_HEADS * MLA_HEAD_PAD).astype(BF16)
    wkv = w_ukv.reshape(MLA_KV_RANK, MLA_HEADS, MLA_NOPE_DIM + MLA_V_DIM)
    pad = ((0, 0), (0, 0), (0, MLA_HEAD_PAD - MLA_NOPE_DIM))
    wk = jnp.pad(wkv[..., :MLA_NOPE_DIM], pad).reshape(MLA_KV_RANK, -1)
    wv = jnp.pad(wkv[..., MLA_NOPE_DIM:], pad).reshape(MLA_KV_RANK, -1)
    wukv = jnp.concatenate([wk, wv], axis=1).astype(BF16)
    return w_b, w_gate, wuq, wukv


def kernel(x, mem, positions, ffn1_norm, ffn1_w_gate, ffn1_w_up, ffn1_w_down, mix_norm, w_in, mla_q_norm, mla_w_uq, mla_kv_norm, mla_w_ukv, mla_w_o, dil_w_o, mem_norm, mem_w_kv, mem_w_o, w_out, ffn2_norm, ffn2_w_gate, ffn2_w_up, ffn2_w_down, final_norm):
    b, s, d = x.shape
    depth = ffn1_norm.shape[0]
    pos = positions.astype(F32).reshape(b, s, 1)
    rope_consts = _rope_consts()
    row = lambda v: v.reshape(1, -1)
    bf = lambda w: w.astype(BF16)
    for l in range(depth):
        x = _ffn(x.reshape(b * s, d), row(ffn1_norm[l]), bf(ffn1_w_gate[l]), bf(ffn1_w_up[l]),
                 bf(ffn1_w_down[l])).reshape(b, s, d)

        w_b, w_gate, wuq, wukv = _prep_mix_weights(w_in[l], mla_w_uq[l], mla_w_ukv[l])
        outs = _mixproj(x, pos, row(mix_norm[l]), w_b, row(mla_q_norm[l]), wuq,
                        row(mla_kv_norm[l]), wukv, rope_consts)
        q_m, k_m, v_m = outs[0:3]
        mq = outs[12]
        o_mla = _mla_attn(q_m, k_m, v_m)
        dil_outs = []
        for g, (window, dil) in enumerate(DIL_GROUPS):
            qg, kg, vg = outs[3 + 3 * g:6 + 3 * g]
            dil_outs.append(_dil_attn(qg, kg, vg, window // (2 * dil)))
        mem_k, mem_v = _mem_kv(mem, row(mem_norm[l]), bf(mem_w_kv[l]))
        x = _mix(x, row(mix_norm[l]), w_gate, o_mla, bf(mla_w_o[l]), dil_outs, bf(dil_w_o[l]),
                 mq, mem_k, mem_v, bf(mem_w_o[l]), bf(w_out[l]))

        last = l == depth - 1
        x = _ffn(x.reshape(b * s, d), row(ffn2_norm[l]), bf(ffn2_w_gate[l]), bf(ffn2_w_up[l]),
                 bf(ffn2_w_down[l]), row(final_norm) if last else None).reshape(b, s, d)
    return x
```

```python
import functools

import numpy as np
import jax
import jax.numpy as jnp
from jax import lax
from jax.experimental import pallas as pl
from jax.experimental.pallas import tpu as pltpu

F32 = jnp.float32
BF16 = jnp.bfloat16

NORM_EPS = 1e-6
NEG_INF = -1e30
ROPE_THETA = 500000.0
LOG2_E = 1.4426950408889634

LANES = 128
VMEM_LIMIT_BYTES = 56 * 1024 * 1024

MLA_HEADS = 8
MLA_Q_RANK = 384
MLA_KV_RANK = 128
MLA_NOPE_DIM = 64
MLA_ROPE_DIM = 32
MLA_V_DIM = 64
MLA_QK_DIM = MLA_NOPE_DIM + MLA_ROPE_DIM
MLA_HEAD_PAD = 128
MLA_ROPE_LANE = 16
ROPE_PARTNER_SHIFT = LANES // 2

DIL_GROUPS = ((128, 1), (512, 4), (2048, 16))
DIL_HEADS = 4
DIL_HEAD_DIM = 64
DIL_ROPE_DIM = 16
DIL_GROUP_WIDTH = DIL_HEADS * DIL_HEAD_DIM
DIL_WIDTH = len(DIL_GROUPS) * DIL_GROUP_WIDTH
DIL_Q_TILE = 128
DIL_K_TILE = 256

MEM_HEADS = 4
MEM_HEAD_DIM = 128
MEM_WIDTH = MEM_HEADS * MEM_HEAD_DIM

N_BRANCHES = 3


def _dot(a, b):
    return jnp.dot(a, b, preferred_element_type=F32)


def _dot_nt(a, b):
    return lax.dot_general(a, b, (((1,), (1,)), ((), ())), preferred_element_type=F32)


def _rms(x, g):
    y = x * lax.rsqrt(jnp.mean(x * x, axis=-1, keepdims=True) + NORM_EPS)
    return y * g


def _const_spec(shape):
    nd = len(shape)
    return pl.BlockSpec(shape, lambda *_: (0,) * nd, pipeline_mode=pl.Buffered(1))


def _params(n_grid):
    return pltpu.CompilerParams(
        dimension_semantics=("arbitrary",) * n_grid,
        vmem_limit_bytes=VMEM_LIMIT_BYTES,
    )


def _ffn_kernel(x_ref, g_ref, wg_ref, wu_ref, wd_ref, *rest, f_chunk, final):
    if final:
        fg_ref, o_ref = rest
    else:
        (o_ref,) = rest
    x = x_ref[...]
    h = _rms(x, g_ref[...]).astype(BF16)
    d_ff = wg_ref.shape[1]
    acc = jnp.zeros(x.shape, F32)
    for c in range(d_ff // f_chunk):
        cols = slice(c * f_chunk, (c + 1) * f_chunk)
        a = _dot(h, wg_ref[:, cols])
        b = _dot(h, wu_ref[:, cols])
        act = (jax.nn.silu(a) * b).astype(BF16)
        acc = acc + _dot(act, wd_ref[cols, :])
    y = x + 0.5 * acc
    if final:
        y = _rms(y, fg_ref[...])
    o_ref[...] = y


def _ffn(x2d, g, wg, wu, wd, final_g=None, *, tm=512, f_chunk=256):
    t, d = x2d.shape
    d_ff = wg.shape[1]
    assert t % tm == 0 and d_ff % f_chunk == 0
    final = final_g is not None
    in_specs = [
        pl.BlockSpec((tm, d), lambda i: (i, 0)),
        _const_spec((1, d)),
        _const_spec((d, d_ff)),
        _const_spec((d, d_ff)),
        _const_spec((d_ff, d)),
    ]
    args = [x2d, g, wg, wu, wd]
    if final:
        in_specs.append(_const_spec((1, d)))
        args.append(final_g)
    return pl.pallas_call(
        functools.partial(_ffn_kernel, f_chunk=f_chunk, final=final),
        grid=(t // tm,),
        in_specs=in_specs,
        out_specs=pl.BlockSpec((tm, d), lambda i: (i, 0)),
        out_shape=jax.ShapeDtypeStruct((t, d), F32),
        compiler_params=_params(1),
        name="ffn_final" if final else "ffn",
    )(*args)


def _rope_apply(x, c, s):
    return x * c + pltpu.roll(x, ROPE_PARTNER_SHIFT, axis=1) * s


def _mixproj_kernel(x_ref, pos_ref, g_ref, w_ref, qn_ref, wuq_ref, kvn_ref, wukv_ref, rc_ref,
                    qm_ref, km_ref, vm_ref,
                    q0_ref, k0_ref, v0_ref, q1_ref, k1_ref, v1_ref, q2_ref, k2_ref, v2_ref,
                    mq_ref, zd_ref):
    tm = x_ref.shape[0]
    h = _rms(x_ref[...], g_ref[...]).astype(BF16)
    pos = pos_ref[...]
    rc = rc_ref[...]

    ang = pos * rc[0:1]
    cos_t, sin_t = jnp.cos(ang), jnp.sin(ang)
    sgn_m, sgn_d = rc[1:2], rc[2:3]
    c_m, s_m = jnp.where(sgn_d != 0.0, 1.0, cos_t), sin_t * sgn_m
    c_d, s_d = jnp.where(sgn_m != 0.0, 1.0, cos_t), sin_t * sgn_d

    q_scale = MLA_QK_DIM ** -0.5 * LOG2_E
    cq_m, sq_m = c_m * q_scale, s_m * q_scale

    z = _dot(h, w_ref[:, 0:640])
    cq = _rms(z[:, 0:MLA_Q_RANK], qn_ref[...]).astype(BF16)
    ckv = _rms(z[:, MLA_Q_RANK:MLA_Q_RANK + MLA_KV_RANK], kvn_ref[...]).astype(BF16)
    kr = _rope_apply(z[:, 512:640], c_m, s_m)
    q = _dot(cq, wuq_ref[...])
    kv = _dot(ckv, wukv_ref[...])
    lane = lax.broadcasted_iota(jnp.int32, (1, LANES), 1)
    ones_col = (lane == MLA_V_DIM).astype(F32)
    hw = MLA_HEADS * MLA_HEAD_PAD
    for hd in range(MLA_HEADS):
        blk = slice(hd * MLA_HEAD_PAD, (hd + 1) * MLA_HEAD_PAD)
        qm_ref[:, blk] = _rope_apply(q[:, blk], cq_m, sq_m).astype(BF16)
        km_ref[:, blk] = (kv[:, blk] + kr).astype(BF16)
        vblk = slice(hw + hd * MLA_HEAD_PAD, hw + (hd + 1) * MLA_HEAD_PAD)
        vm_ref[:, blk] = (kv[:, vblk] + ones_col).astype(BF16)

    d_scale = DIL_HEAD_DIM ** -0.5 * LOG2_E
    cq_d, sq_d = c_d * d_scale, s_d * d_scale
    zd = _dot(h, w_ref[:, 640:640 + 3 * DIL_WIDTH])
    nblk = DIL_WIDTH // LANES
    for j in range(nblk):
        blk = slice(j * LANES, (j + 1) * LANES)
        zd_ref[j] = _rope_apply(zd[:, blk], cq_d, sq_d)
        kblk = slice(DIL_WIDTH + j * LANES, DIL_WIDTH + (j + 1) * LANES)
        zd_ref[nblk + j] = _rope_apply(zd[:, kblk], c_d, s_d)
        vblk = slice(2 * DIL_WIDTH + j * LANES, 2 * DIL_WIDTH + (j + 1) * LANES)
        zd_ref[2 * nblk + j] = zd[:, vblk]

    outs = ((q0_ref, k0_ref, v0_ref), (q1_ref, k1_ref, v1_ref), (q2_ref, k2_ref, v2_ref))
    gblk = DIL_GROUP_WIDTH // LANES
    for g, (_, dil) in enumerate(DIL_GROUPS):
        for part in range(3):
            o_ref = outs[g][part]
            for jj in range(gblk):
                j = part * nblk + g * gblk + jj
                cols = slice(jj * LANES, (jj + 1) * LANES)
                if dil == 1:
                    o_ref[0, :, cols] = zd_ref[j].astype(BF16)
                else:
                    for r in range(dil):
                        o_ref[r, :, cols] = zd_ref[j, pl.ds(r, tm // dil, stride=dil), :].astype(BF16)

    mq = _dot(h, w_ref[:, 640 + 3 * DIL_WIDTH:])
    mq_ref[...] = (mq * (MEM_HEAD_DIM ** -0.5)).astype(BF16)


def _mixproj(x1, pos, g, w_b, qn, wuq, kvn, wukv, rope_consts, *, tm=512):
    b, s, d = x1.shape
    assert s % tm == 0 and tm % 16 == 0
    hw = MLA_HEADS * MLA_HEAD_PAD
    tok = lambda width: pl.BlockSpec((None, tm, width), lambda bi, i: (bi, i, 0))
    in_specs = [
        tok(d),
        tok(1),
        _const_spec(g.shape),
        _const_spec(w_b.shape),
        _const_spec(qn.shape),
        _const_spec(wuq.shape),
        _const_spec(kvn.shape),
        _const_spec(wukv.shape),
        _const_spec(rope_consts.shape),
    ]
    out_shapes = [jax.ShapeDtypeStruct((b, s, hw), BF16)] * 3
    out_specs = [tok(hw)] * 3
    for _, dil in DIL_GROUPS:
        for _ in range(3):
            out_shapes.append(jax.ShapeDtypeStruct((b, dil, s // dil, DIL_GROUP_WIDTH), BF16))
            out_specs.append(pl.BlockSpec((None, dil, tm // dil, DIL_GROUP_WIDTH),
                                          lambda bi, i: (bi, 0, i, 0)))
    out_shapes.append(jax.ShapeDtypeStruct((b, s, MEM_WIDTH), BF16))
    out_specs.append(tok(MEM_WIDTH))
    return pl.pallas_call(
        _mixproj_kernel,
        grid=(b, s // tm),
        in_specs=in_specs,
        out_specs=out_specs,
        out_shape=out_shapes,
        scratch_shapes=[pltpu.VMEM((3 * DIL_WIDTH // LANES, tm, LANES), F32)],
        compiler_params=_params(2),
        name="mixproj",
    )(x1, pos, g, w_b, qn, wuq, kvn, wukv, rope_consts)


def _mla_attn_kernel(q_ref, k_ref, v_ref, o_ref, *, tk, unroll):
    tq = q_ref.shape[0]
    s_len = k_ref.shape[0]
    heads = q_ref.shape[1] // MLA_HEAD_PAD
    blks = [slice(hd * MLA_HEAD_PAD, (hd + 1) * MLA_HEAD_PAD) for hd in range(heads)]

    def body(j, carry):
        rows = pl.ds(pl.multiple_of(j * tk, tk), tk)
        new = []
        for blk, (m, acc) in zip(blks, carry):
            s = _dot_nt(q_ref[:, blk], k_ref[rows, blk])
            m_new = jnp.maximum(m, jnp.max(s, axis=-1, keepdims=True))
            p = jnp.exp2(s - m_new).astype(BF16)
            acc = jnp.exp2(m - m_new) * acc + _dot(p, v_ref[rows, blk])
            new.append((m_new, acc))
        return tuple(new)

    init = tuple((jnp.full((tq, 1), NEG_INF, F32), jnp.zeros((tq, MLA_HEAD_PAD), F32))
                 for _ in range(heads))
    final = lax.fori_loop(0, s_len // tk, body, init, unroll=unroll)
    outs = [acc * (1.0 / acc[:, MLA_V_DIM:MLA_V_DIM + 1]) for _, acc in final]
    lane = lax.broadcasted_iota(jnp.int32, (1, LANES), 1)
    for pr in range(heads // 2):
        even, odd = outs[2 * pr], outs[2 * pr + 1]
        pair = jnp.where(lane < MLA_V_DIM, even, pltpu.roll(odd, MLA_V_DIM, axis=1))
        o_ref[:, pr * LANES:(pr + 1) * LANES] = pair.astype(BF16)


def _mla_attn(q, k, v, *, tq=1024, tk=2048, heads_per_step=2, unroll=1):
    b, s, hw = q.shape
    assert s % tq == 0 and s % tk == 0 and heads_per_step % 2 == 0
    wblk = heads_per_step * MLA_HEAD_PAD
    oblk = heads_per_step * MLA_V_DIM
    return pl.pallas_call(
        functools.partial(_mla_attn_kernel, tk=tk, unroll=unroll),
        grid=(b, hw // wblk, s // tq),
        in_specs=[
            pl.BlockSpec((None, tq, wblk), lambda bi, hi, i: (bi, i, hi)),
            pl.BlockSpec((None, s, wblk), lambda bi, hi, i: (bi, 0, hi)),
            pl.BlockSpec((None, s, wblk), lambda bi, hi, i: (bi, 0, hi)),
        ],
        out_specs=pl.BlockSpec((None, tq, oblk), lambda bi, hi, i: (bi, i, hi)),
        out_shape=jax.ShapeDtypeStruct((b, s, MLA_HEADS * MLA_V_DIM), BF16),
        compiler_params=_params(3),
        name="mla_attn",
    )(q, k, v)


def _lane_mask(lane, flags):
    edges = np.flatnonzero(np.diff(np.concatenate([[0], flags.astype(np.int8), [0]])))
    mask = None
    for lo, hi in zip(edges[0::2], edges[1::2]):
        run = (lane >= int(lo)) & (lane < int(hi))
        mask = run if mask is None else (mask | run)
    return mask


def _dil_attn_kernel(q_ref, k_ref, v_ref, o_ref, lse_ref, *, n_side):
    n_cls, tq_all, _ = q_ref.shape
    length = k_ref.shape[1]
    i = pl.program_id(2)
    lane = lax.broadcasted_iota(jnp.int32, (1, LANES), 1)
    low = lane < DIL_HEAD_DIM
    q_first = _lane_mask(lane, _dil_lane_map() < DIL_HEAD_DIM)
    diff = (lax.broadcasted_iota(jnp.int32, (DIL_Q_TILE, DIL_K_TILE), 1)
            - lax.broadcasted_iota(jnp.int32, (DIL_Q_TILE, DIL_K_TILE), 0))
    ones_blk = jnp.ones((DIL_K_TILE, LANES), BF16)
    for t in range(tq_all // DIL_Q_TILE):
        qs = i * tq_all + t * DIL_Q_TILE
        ks = pl.multiple_of(jnp.clip(qs - n_side, 0, length - DIL_K_TILE), n_side)
        bias = jnp.where(jnp.abs(diff + (ks - qs)) <= n_side, 0.0, NEG_INF)
        bias2 = jnp.concatenate([bias, bias], axis=0)
        rows = slice(t * DIL_Q_TILE, (t + 1) * DIL_Q_TILE)
        for c in range(n_cls):
            for pr in range(DIL_GROUP_WIDTH // LANES):
                cols = slice(pr * LANES, (pr + 1) * LANES)
                qp = q_ref[c, rows, cols]
                kp = k_ref[c, pl.ds(ks, DIL_K_TILE), cols]
                vp = v_ref[c, pl.ds(ks, DIL_K_TILE), cols]
                zero = jnp.zeros_like(qp)
                q2 = jnp.concatenate([jnp.where(q_first, qp, zero), jnp.where(q_first, zero, qp)], axis=0)
                s = _dot_nt(q2, kp) + bias2
                m = jnp.max(s, axis=-1, keepdims=True)
                p = jnp.exp2(s - m).astype(BF16)
                pv = _dot(p, jnp.concatenate([vp, ones_blk], axis=1))
                pick = lambda a: jnp.where(low, a[:DIL_Q_TILE], a[DIL_Q_TILE:])
                num, den = pick(pv[:, :LANES]), pick(pv[:, LANES:])
                o_ref[c, rows, cols] = (num * (1.0 / den)).astype(BF16)
                lse_ref[c, rows, cols] = pick(jnp.broadcast_to(m, (2 * DIL_Q_TILE, LANES))) + jnp.log2(den)


def _dil_attn(q, k, v, n_side, *, subtiles_per_step=8):
    b, dil, length, w = q.shape
    assert w == DIL_GROUP_WIDTH and n_side == DIL_Q_TILE // 2
    assert DIL_K_TILE == DIL_Q_TILE + 2 * n_side and length >= DIL_K_TILE
    tq = min(length, subtiles_per_step * DIL_Q_TILE)
    n_cls = min(dil, (subtiles_per_step * DIL_Q_TILE) // tq)
    assert length % tq == 0 and tq % DIL_Q_TILE == 0 and dil % n_cls == 0
    q_spec = pl.BlockSpec((None, n_cls, tq, w), lambda bi, r, i: (bi, r, i, 0))
    kv_spec = pl.BlockSpec((None, n_cls, length, w), lambda bi, r, i: (bi, r, 0, 0))
    return pl.pallas_call(
        functools.partial(_dil_attn_kernel, n_side=n_side),
        grid=(b, dil // n_cls, length // tq),
        in_specs=[q_spec, kv_spec, kv_spec],
        out_specs=[q_spec, q_spec],
        out_shape=[jax.ShapeDtypeStruct(q.shape, BF16), jax.ShapeDtypeStruct(q.shape, F32)],
        compiler_params=_params(3),
        name=f"dil_attn_d{dil}",
    )(q, k, v)


def _mem_kv_kernel(mem_ref, g_ref, w_ref, k_ref, v_ref):
    hm = _rms(mem_ref[...], g_ref[...]).astype(BF16)
    kv = _dot(hm, w_ref[...])
    k_ref[...] = kv[:, :MEM_WIDTH].astype(BF16)
    v_ref[...] = kv[:, MEM_WIDTH:].astype(BF16)


def _mem_kv(mem, g, w_kv):
    b, m, d = mem.shape
    spec_out = pl.BlockSpec((None, m, MEM_WIDTH), lambda bi: (bi, 0, 0))
    return pl.pallas_call(
        _mem_kv_kernel,
        grid=(b,),
        in_specs=[pl.BlockSpec((None, m, d), lambda bi: (bi, 0, 0)),
                  _const_spec(g.shape), _const_spec(w_kv.shape)],
        out_specs=[spec_out, spec_out],
        out_shape=[jax.ShapeDtypeStruct((b, m, MEM_WIDTH), BF16)] * 2,
        compiler_params=_params(1),
        name="mem_kv",
    )(mem, g, w_kv)


def _mix_kernel(x_ref, g_ref, wgate_ref, omla_ref, womla_ref,
                o0_ref, l0_ref, o1_ref, l1_ref, o2_ref, l2_ref, wodil_ref,
                mq_ref, mk_ref, mv_ref, womem_ref, wout_ref,
                out_ref, so1_ref, sl1_ref, so2_ref, sl2_ref):
    x = x_ref[...]
    d = x.shape[1]
    h = _rms(x, g_ref[...]).astype(BF16)

    o_mem = []
    for hd in range(MEM_HEADS):
        blk = slice(hd * MEM_HEAD_DIM, (hd + 1) * MEM_HEAD_DIM)
        s = _dot_nt(mq_ref[:, blk], mk_ref[:, blk])
        m = jnp.max(s, axis=-1, keepdims=True)
        p = jnp.exp(s - m)
        l = jnp.sum(p, axis=-1, keepdims=True)
        o_mem.append((_dot(p.astype(BF16), mv_ref[:, blk]) * (1.0 / l)).astype(BF16))
    y_mem = _dot(jnp.concatenate(o_mem, axis=1), womem_ref[...])

    gblk = DIL_GROUP_WIDTH // LANES
    for o_ref, l_ref, so_ref, sl_ref in ((o1_ref, l1_ref, so1_ref, sl1_ref),
                                         (o2_ref, l2_ref, so2_ref, sl2_ref)):
        dil, n = o_ref.shape[0], o_ref.shape[1]
        for r in range(dil):
            rows = pl.ds(r, n, stride=dil)
            for jj in range(gblk):
                cols = slice(jj * LANES, (jj + 1) * LANES)
                so_ref[jj, rows, :] = o_ref[r, :, cols].astype(F32)
                sl_ref[jj, rows, :] = l_ref[r, :, cols]
    o_parts = []
    for jj in range(gblk):
        cols = slice(jj * LANES, (jj + 1) * LANES)
        l0, l1, l2 = l0_ref[:, cols], sl1_ref[jj], sl2_ref[jj]
        lmax = jnp.maximum(jnp.maximum(l0, l1), l2)
        w0, w1, w2 = jnp.exp2(l0 - lmax), jnp.exp2(l1 - lmax), jnp.exp2(l2 - lmax)
        o_sum = w0 * o0_ref[:, cols].astype(F32) + w1 * so1_ref[jj] + w2 * so2_ref[jj]
        o_parts.append((o_sum * (1.0 / (w0 + w1 + w2))).astype(BF16))
    y_dil = _dot(jnp.concatenate(o_parts, axis=1), wodil_ref[...])

    y_mla = _dot(omla_ref[...], womla_ref[...])

    mixed = jax.nn.sigmoid(_dot(h, wgate_ref[:, 0:d])) * y_mla
    mixed = mixed + jax.nn.sigmoid(_dot(h, wgate_ref[:, d:2 * d])) * y_dil
    mixed = mixed + jax.nn.sigmoid(_dot(h, wgate_ref[:, 2 * d:3 * d])) * y_mem
    out_ref[...] = x + _dot(mixed.astype(BF16), wout_ref[...])


def _mix(x1, g, w_gate, o_mla, wo_mla, dil_outs, wo_dil, mq, mem_k, mem_v, wo_mem, w_out, *, tm=512):
    b, s, d = x1.shape
    assert s % tm == 0
    n_mem = mem_k.shape[1]
    tok = lambda width: pl.BlockSpec((None, tm, width), lambda bi, i: (bi, i, 0))
    in_specs = [tok(d), _const_spec(g.shape), _const_spec(w_gate.shape),
                tok(o_mla.shape[2]), _const_spec(wo_mla.shape)]
    args = [x1, g, w_gate, o_mla, wo_mla]
    for (o_g, lse_g), (_, dil) in zip(dil_outs, DIL_GROUPS):
        if dil == 1:
            spec = tok(DIL_GROUP_WIDTH)
            o_g, lse_g = o_g.reshape(b, s, DIL_GROUP_WIDTH), lse_g.reshape(b, s, DIL_GROUP_WIDTH)
        else:
            spec = pl.BlockSpec((None, dil, tm // dil, DIL_GROUP_WIDTH), lambda bi, i: (bi, 0, i, 0))
        in_specs += [spec, spec]
        args += [o_g, lse_g]
    in_specs += [_const_spec(wo_dil.shape), tok(MEM_WIDTH),
                 pl.BlockSpec((None, n_mem, MEM_WIDTH), lambda bi, i: (bi, 0, 0)),
                 pl.BlockSpec((None, n_mem, MEM_WIDTH), lambda bi, i: (bi, 0, 0)),
                 _const_spec(wo_mem.shape), _const_spec(w_out.shape)]
    args += [wo_dil, mq, mem_k, mem_v, wo_mem, w_out]
    return pl.pallas_call(
        _mix_kernel,
        grid=(b, s // tm),
        in_specs=in_specs,
        out_specs=tok(d),
        out_shape=jax.ShapeDtypeStruct((b, s, d), F32),
        scratch_shapes=[pltpu.VMEM((DIL_GROUP_WIDTH // LANES, tm, LANES), F32)] * 4,
        compiler_params=_params(2),
        name="mix",
    )(*args)


def _mla_lane_map():
    half = MLA_ROPE_DIM // 2
    r1 = MLA_ROPE_LANE
    lanes = np.full(LANES, MLA_QK_DIM)
    lanes[r1:r1 + half] = MLA_NOPE_DIM + np.arange(half)
    lanes[r1 + ROPE_PARTNER_SHIFT:r1 + ROPE_PARTNER_SHIFT + half] = MLA_NOPE_DIM + half + np.arange(half)
    free = [i for i in range(MLA_QK_DIM) if lanes[i] == MLA_QK_DIM]
    lanes[free] = np.arange(MLA_NOPE_DIM)
    return lanes


def _dil_lane_map():
    half = DIL_ROPE_DIM // 2
    a, b = np.arange(DIL_HEAD_DIM), DIL_HEAD_DIM + np.arange(DIL_HEAD_DIM)
    return np.concatenate([a[:half], b[:half], a[2 * half:], a[half:2 * half], b[half:2 * half], b[2 * half:]])


def _rope_consts():
    half_m, half_d = MLA_ROPE_DIM // 2, DIL_ROPE_DIM // 2
    f_m = ROPE_THETA ** (-2.0 * jnp.arange(half_m, dtype=F32) / MLA_ROPE_DIM)
    f_d = ROPE_THETA ** (-2.0 * jnp.arange(half_d, dtype=F32) / DIL_ROPE_DIM)
    rows = jnp.zeros((8, LANES), F32)
    for base, sign in ((0, -1.0), (ROPE_PARTNER_SHIFT, 1.0)):
        m0 = base + MLA_ROPE_LANE
        rows = rows.at[0, m0:m0 + half_m].set(f_m).at[1, m0:m0 + half_m].set(sign)
        rows = rows.at[0, base:base + half_d].set(f_d).at[0, base + half_d:base + 2 * half_d].set(f_d)
        rows = rows.at[2, base:base + 2 * half_d].set(sign)
    return rows


def _prep_mix_weights(w_in, w_uq, w_ukv):
    d = w_in.shape[0]
    n_lat = MLA_Q_RANK + MLA_KV_RANK
    kr0 = n_lat
    dil0 = kr0 + MLA_ROPE_DIM
    mq0 = dil0 + 3 * DIL_WIDTH
    gate0 = mq0 + MEM_WIDTH
    mla_map = _mla_lane_map()
    kr_cols = jnp.concatenate([jnp.zeros((d, MLA_NOPE_DIM), F32), w_in[:, kr0:dil0],
                               jnp.zeros((d, 1), F32)], axis=1)
    kr_map = np.where(mla_map >= MLA_NOPE_DIM, mla_map, MLA_QK_DIM)
    kr_block = kr_cols[:, kr_map]
    pair_map = _dil_lane_map()
    n_pairs = 2 * DIL_WIDTH // LANES
    qk_map = (np.arange(n_pairs)[:, None] * LANES + pair_map[None, :]).reshape(-1)
    dil_qk = w_in[:, dil0:dil0 + 2 * DIL_WIDTH][:, qk_map]
    w_b = jnp.concatenate([w_in[:, :n_lat], kr_block, dil_qk, w_in[:, dil0 + 2 * DIL_WIDTH:gate0]],
                          axis=1).astype(BF16)
    w_gate = w_in[:, gate0:].astype(BF16)
    wuq = jnp.pad(w_uq.reshape(MLA_Q_RANK, MLA_HEADS, MLA_QK_DIM), ((0, 0), (0, 0), (0, 1)))
    wuq = wuq[:, :, mla_map].reshape(MLA_Q_RANK, MLA_HEADS * MLA_HEAD_PAD).astype(BF16)
    wkv = w_ukv.reshape(MLA_KV_RANK, MLA_HEADS, MLA_NOPE_DIM + MLA_V_DIM)
    k_map = np.where(mla_map < MLA_NOPE_DIM, mla_map, MLA_NOPE_DIM)
    wk = jnp.pad(wkv[..., :MLA_NOPE_DIM], ((0, 0), (0, 0), (0, 1)))[:, :, k_map].reshape(MLA_KV_RANK, -1)
    wv = jnp.pad(wkv[..., MLA_NOPE_DIM:], ((0, 0), (0, 0), (0, MLA_HEAD_PAD - MLA_V_DIM)))
    wukv = jnp.concatenate([wk, wv.reshape(MLA_KV_RANK, -1)], axis=1).astype(BF16)
    return w_b, w_gate, wuq, wukv


def kernel(x, mem, positions, ffn1_norm, ffn1_w_gate, ffn1_w_up, ffn1_w_down, mix_norm, w_in, mla_q_norm, mla_w_uq, mla_kv_norm, mla_w_ukv, mla_w_o, dil_w_o, mem_norm, mem_w_kv, mem_w_o, w_out, ffn2_norm, ffn2_w_gate, ffn2_w_up, ffn2_w_down, final_norm):
    b, s, d = x.shape
    depth = ffn1_norm.shape[0]
    pos = positions.astype(F32).reshape(b, s, 1)
    rope_consts = _rope_consts()
    row = lambda v: v.reshape(1, -1)
    bf = lambda w: w.astype(BF16)
    for l in range(depth):
        x = _ffn(x.reshape(b * s, d), row(ffn1_norm[l]), bf(ffn1_w_gate[l]), bf(ffn1_w_up[l]),
                 bf(ffn1_w_down[l])).reshape(b, s, d)

        w_b, w_gate, wuq, wukv = _prep_mix_weights(w_in[l], mla_w_uq[l], mla_w_ukv[l])
        outs = _mixproj(x, pos, row(mix_norm[l]), w_b, row(mla_q_norm[l]), wuq,
                        row(mla_kv_norm[l]), wukv, rope_consts)
        q_m, k_m, v_m = outs[0:3]
        mq = outs[12]
        o_mla = _mla_attn(q_m, k_m, v_m)
        dil_outs = []
        for g, (window, dil) in enumerate(DIL_GROUPS):
            qg, kg, vg = outs[3 + 3 * g:6 + 3 * g]
            dil_outs.append(_dil_attn(qg, kg, vg, window // (2 * dil)))
        mem_k, mem_v = _mem_kv(mem, row(mem_norm[l]), bf(mem_w_kv[l]))
        x = _mix(x, row(mix_norm[l]), w_gate, o_mla, bf(mla_w_o[l]), dil_outs, bf(dil_w_o[l]),
                 mq, mem_k, mem_v, bf(mem_w_o[l]), bf(w_out[l]))

        last = l == depth - 1
        x = _ffn(x.reshape(b * s, d), row(ffn2_norm[l]), bf(ffn2_w_gate[l]), bf(ffn2_w_up[l]),
                 bf(ffn2_w_down[l]), row(final_norm) if last else None).reshape(b, s, d)
    return x
```

```python
import functools

import numpy as np
import jax
import jax.numpy as jnp
from jax import lax
from jax.experimental import pallas as pl
from jax.experimental.pallas import tpu as pltpu

F32 = jnp.float32
BF16 = jnp.bfloat16

NORM_EPS = 1e-6
NEG_INF = -1e30
ROPE_THETA = 500000.0
LOG2_E = 1.4426950408889634

LANES = 128
MXU_COLS = 256
VMEM_LIMIT_BYTES = 56 * 1024 * 1024

MLA_HEADS = 8
MLA_Q_RANK = 384
MLA_KV_RANK = 128
MLA_NOPE_DIM = 64
MLA_ROPE_DIM = 32
MLA_V_DIM = 64
MLA_QK_DIM = MLA_NOPE_DIM + MLA_ROPE_DIM
MLA_HEAD_PAD = 128
MLA_ROPE_LANE = 16
ROPE_PARTNER_SHIFT = LANES // 2

DIL_GROUPS = ((128, 1), (512, 4), (2048, 16))
DIL_HEADS = 4
DIL_HEAD_DIM = 64
DIL_ROPE_DIM = 16
DIL_GROUP_WIDTH = DIL_HEADS * DIL_HEAD_DIM
DIL_WIDTH = len(DIL_GROUPS) * DIL_GROUP_WIDTH
DIL_Q_TILE = 128
DIL_K_TILE = 256

MEM_HEADS = 4
MEM_HEAD_DIM = 128
MEM_WIDTH = MEM_HEADS * MEM_HEAD_DIM

N_BRANCHES = 3


def _dot(a, b):
    return jnp.dot(a, b, preferred_element_type=F32)


def _dot_nt(a, b):
    return lax.dot_general(a, b, (((1,), (1,)), ((), ())), preferred_element_type=F32)


def _rms(x, g):
    y = x * lax.rsqrt(jnp.mean(x * x, axis=-1, keepdims=True) + NORM_EPS)
    return y * g


def _const_spec(shape):
    nd = len(shape)
    return pl.BlockSpec(shape, lambda *_: (0,) * nd, pipeline_mode=pl.Buffered(1))


def _params(n_grid):
    return pltpu.CompilerParams(
        dimension_semantics=("arbitrary",) * n_grid,
        vmem_limit_bytes=VMEM_LIMIT_BYTES,
    )


def _ffn_kernel(x_ref, g_ref, wg_ref, wu_ref, wd_ref, *rest, f_chunk, final):
    if final:
        fg_ref, o_ref = rest
    else:
        (o_ref,) = rest
    x = x_ref[...]
    h = _rms(x, g_ref[...]).astype(BF16)
    d_ff = wg_ref.shape[1]
    acc = jnp.zeros(x.shape, F32)
    for c in range(d_ff // f_chunk):
        cols = slice(c * f_chunk, (c + 1) * f_chunk)
        a = _dot(h, wg_ref[:, cols])
        b = _dot(h, wu_ref[:, cols])
        act = (jax.nn.silu(a) * b).astype(BF16)
        acc = acc + _dot(act, wd_ref[cols, :])
    y = x + 0.5 * acc
    if final:
        y = _rms(y, fg_ref[...])
    o_ref[...] = y


def _ffn(x2d, g, wg, wu, wd, final_g=None, *, tm=512, f_chunk=256):
    t, d = x2d.shape
    d_ff = wg.shape[1]
    assert t % tm == 0 and d_ff % f_chunk == 0
    final = final_g is not None
    in_specs = [
        pl.BlockSpec((tm, d), lambda i: (i, 0)),
        _const_spec((1, d)),
        _const_spec((d, d_ff)),
        _const_spec((d, d_ff)),
        _const_spec((d_ff, d)),
    ]
    args = [x2d, g, wg, wu, wd]
    if final:
        in_specs.append(_const_spec((1, d)))
        args.append(final_g)
    return pl.pallas_call(
        functools.partial(_ffn_kernel, f_chunk=f_chunk, final=final),
        grid=(t // tm,),
        in_specs=in_specs,
        out_specs=pl.BlockSpec((tm, d), lambda i: (i, 0)),
        out_shape=jax.ShapeDtypeStruct((t, d), F32),
        compiler_params=_params(1),
        name="ffn_final" if final else "ffn",
    )(*args)


def _rope_apply(x, c, s):
    return x * c + pltpu.roll(x, ROPE_PARTNER_SHIFT, axis=1) * s


def _mixproj_kernel(x_ref, pos_ref, g_ref, w_ref, qn_ref, wuq_ref, kvn_ref, wukv_ref, rc_ref,
                    qm_ref, km_ref, vm_ref,
                    q0_ref, k0_ref, v0_ref, q1_ref, k1_ref, v1_ref, q2_ref, k2_ref, v2_ref,
                    mq_ref, zd_ref):
    tm = x_ref.shape[0]
    h = _rms(x_ref[...], g_ref[...]).astype(BF16)
    pos = pos_ref[...]
    rc = rc_ref[...]

    ang = pos * rc[0:1]
    cos_t, sin_t = jnp.cos(ang), jnp.sin(ang)
    sgn_m, sgn_d = rc[1:2], rc[2:3]
    c_m, s_m = jnp.where(sgn_d != 0.0, 1.0, cos_t), sin_t * sgn_m
    c_d, s_d = jnp.where(sgn_m != 0.0, 1.0, cos_t), sin_t * sgn_d

    q_scale = MLA_QK_DIM ** -0.5 * LOG2_E
    cq_m, sq_m = c_m * q_scale, s_m * q_scale

    z = _dot(h, w_ref[:, 0:640])
    cq = _rms(z[:, 0:MLA_Q_RANK], qn_ref[...]).astype(BF16)
    ckv = _rms(z[:, MLA_Q_RANK:MLA_Q_RANK + MLA_KV_RANK], kvn_ref[...]).astype(BF16)
    kr = _rope_apply(z[:, 512:640], c_m, s_m)

    d_scale = DIL_HEAD_DIM ** -0.5 * LOG2_E
    cq_d, sq_d = c_d * d_scale, s_d * d_scale
    nblk = DIL_WIDTH // LANES
    for c0 in range(0, 3 * DIL_WIDTH, MXU_COLS):
        zd_c = _dot(h, w_ref[:, 640 + c0:640 + c0 + MXU_COLS])
        for o in range(0, MXU_COLS, LANES):
            j = (c0 + o) // LANES
            part = zd_c[:, o:o + LANES]
            if j < nblk:
                zd_ref[j] = _rope_apply(part, cq_d, sq_d)
            elif j < 2 * nblk:
                zd_ref[j] = _rope_apply(part, c_d, s_d)
            else:
                zd_ref[j] = part

    outs = ((q0_ref, k0_ref, v0_ref), (q1_ref, k1_ref, v1_ref), (q2_ref, k2_ref, v2_ref))
    gblk = DIL_GROUP_WIDTH // LANES
    for g, (_, dil) in enumerate(DIL_GROUPS):
        for part in range(3):
            o_ref = outs[g][part]
            for jj in range(gblk):
                j = part * nblk + g * gblk + jj
                cols = slice(jj * LANES, (jj + 1) * LANES)
                if dil == 1:
                    o_ref[0, :, cols] = zd_ref[j].astype(BF16)
                else:
                    for r in range(dil):
                        o_ref[r, :, cols] = zd_ref[j, pl.ds(r, tm // dil, stride=dil), :].astype(BF16)

    mq0 = 640 + 3 * DIL_WIDTH
    for c0 in range(0, MEM_WIDTH, MXU_COLS):
        mq_c = _dot(h, w_ref[:, mq0 + c0:mq0 + c0 + MXU_COLS])
        mq_ref[:, c0:c0 + MXU_COLS] = (mq_c * (MEM_HEAD_DIM ** -0.5)).astype(BF16)

    lane = lax.broadcasted_iota(jnp.int32, (1, LANES), 1)
    ones_col = (lane == MLA_V_DIM).astype(F32)
    hw = MLA_HEADS * MLA_HEAD_PAD
    for c0 in range(0, hw, MXU_COLS):
        q_c = _dot(cq, wuq_ref[:, c0:c0 + MXU_COLS])
        k_c = _dot(ckv, wukv_ref[:, c0:c0 + MXU_COLS])
        v_c = _dot(ckv, wukv_ref[:, hw + c0:hw + c0 + MXU_COLS])
        for o in range(0, MXU_COLS, MLA_HEAD_PAD):
            src, dst = slice(o, o + MLA_HEAD_PAD), slice(c0 + o, c0 + o + MLA_HEAD_PAD)
            qm_ref[:, dst] = _rope_apply(q_c[:, src], cq_m, sq_m).astype(BF16)
            km_ref[:, dst] = (k_c[:, src] + kr).astype(BF16)
            vm_ref[:, dst] = (v_c[:, src] + ones_col).astype(BF16)


def _mixproj(x1, pos, g, w_b, qn, wuq, kvn, wukv, rope_consts, *, tm=512):
    b, s, d = x1.shape
    assert s % tm == 0 and tm % 16 == 0
    hw = MLA_HEADS * MLA_HEAD_PAD
    tok = lambda width: pl.BlockSpec((None, tm, width), lambda bi, i: (bi, i, 0))
    in_specs = [
        tok(d),
        tok(1),
        _const_spec(g.shape),
        _const_spec(w_b.shape),
        _const_spec(qn.shape),
        _const_spec(wuq.shape),
        _const_spec(kvn.shape),
        _const_spec(wukv.shape),
        _const_spec(rope_consts.shape),
    ]
    out_shapes = [jax.ShapeDtypeStruct((b, s, hw), BF16)] * 3
    out_specs = [tok(hw)] * 3
    for _, dil in DIL_GROUPS:
        for _ in range(3):
            out_shapes.append(jax.ShapeDtypeStruct((b, dil, s // dil, DIL_GROUP_WIDTH), BF16))
            out_specs.append(pl.BlockSpec((None, dil, tm // dil, DIL_GROUP_WIDTH),
                                          lambda bi, i: (bi, 0, i, 0)))
    out_shapes.append(jax.ShapeDtypeStruct((b, s, MEM_WIDTH), BF16))
    out_specs.append(tok(MEM_WIDTH))
    return pl.pallas_call(
        _mixproj_kernel,
        grid=(b, s // tm),
        in_specs=in_specs,
        out_specs=out_specs,
        out_shape=out_shapes,
        scratch_shapes=[pltpu.VMEM((3 * DIL_WIDTH // LANES, tm, LANES), F32)],
        compiler_params=_params(2),
        name="mixproj",
    )(x1, pos, g, w_b, qn, wuq, kvn, wukv, rope_consts)


def _mla_attn_kernel(q_ref, k_ref, v_ref, o_ref, *, tk, unroll):
    tq = q_ref.shape[0]
    s_len = k_ref.shape[0]
    heads = q_ref.shape[1] // MLA_HEAD_PAD
    blks = [slice(hd * MLA_HEAD_PAD, (hd + 1) * MLA_HEAD_PAD) for hd in range(heads)]

    def body(j, carry):
        rows = pl.ds(pl.multiple_of(j * tk, tk), tk)
        new = []
        for blk, (m, acc) in zip(blks, carry):
            s = _dot_nt(q_ref[:, blk], k_ref[rows, blk])
            m_new = jnp.maximum(m, jnp.max(s, axis=-1, keepdims=True))
            p = jnp.exp2(s - m_new).astype(BF16)
            acc = jnp.exp2(m - m_new) * acc + _dot(p, v_ref[rows, blk])
            new.append((m_new, acc))
        return tuple(new)

    init = tuple((jnp.full((tq, 1), NEG_INF, F32), jnp.zeros((tq, MLA_HEAD_PAD), F32))
                 for _ in range(heads))
    final = lax.fori_loop(0, s_len // tk, body, init, unroll=unroll)
    outs = [acc * (1.0 / acc[:, MLA_V_DIM:MLA_V_DIM + 1]) for _, acc in final]
    lane = lax.broadcasted_iota(jnp.int32, (1, LANES), 1)
    for pr in range(heads // 2):
        even, odd = outs[2 * pr], outs[2 * pr + 1]
        pair = jnp.where(lane < MLA_V_DIM, even, pltpu.roll(odd, MLA_V_DIM, axis=1))
        o_ref[:, pr * LANES:(pr + 1) * LANES] = pair.astype(BF16)


def _mla_attn(q, k, v, *, tq=1024, tk=2048, heads_per_step=2, unroll=2):
    b, s, hw = q.shape
    assert s % tq == 0 and s % tk == 0 and heads_per_step % 2 == 0
    wblk = heads_per_step * MLA_HEAD_PAD
    oblk = heads_per_step * MLA_V_DIM
    return pl.pallas_call(
        functools.partial(_mla_attn_kernel, tk=tk, unroll=unroll),
        grid=(b, hw // wblk, s // tq),
        in_specs=[
            pl.BlockSpec((None, tq, wblk), lambda bi, hi, i: (bi, i, hi)),
            pl.BlockSpec((None, s, wblk), lambda bi, hi, i: (bi, 0, hi)),
            pl.BlockSpec((None, s, wblk), lambda bi, hi, i: (bi, 0, hi)),
        ],
        out_specs=pl.BlockSpec((None, tq, oblk), lambda bi, hi, i: (bi, i, hi)),
        out_shape=jax.ShapeDtypeStruct((b, s, MLA_HEADS * MLA_V_DIM), BF16),
        compiler_params=_params(3),
        name="mla_attn",
    )(q, k, v)


def _lane_mask(lane, flags):
    edges = np.flatnonzero(np.diff(np.concatenate([[0], flags.astype(np.int8), [0]])))
    mask = None
    for lo, hi in zip(edges[0::2], edges[1::2]):
        run = (lane >= int(lo)) & (lane < int(hi))
        mask = run if mask is None else (mask | run)
    return mask


def _dil_attn_kernel(q_ref, k_ref, v_ref, o_ref, lse_ref, *, n_side):
    n_cls, tq_all, _ = q_ref.shape
    length = k_ref.shape[1]
    i = pl.program_id(2)
    lane = lax.broadcasted_iota(jnp.int32, (1, LANES), 1)
    low = lane < DIL_HEAD_DIM
    q_first = _lane_mask(lane, _dil_lane_map() < DIL_HEAD_DIM)
    diff = (lax.broadcasted_iota(jnp.int32, (DIL_Q_TILE, DIL_K_TILE), 1)
            - lax.broadcasted_iota(jnp.int32, (DIL_Q_TILE, DIL_K_TILE), 0))
    ones_blk = jnp.ones((DIL_K_TILE, LANES), BF16)
    for t in range(tq_all // DIL_Q_TILE):
        qs = i * tq_all + t * DIL_Q_TILE
        ks = pl.multiple_of(jnp.clip(qs - n_side, 0, length - DIL_K_TILE), n_side)
        bias = jnp.where(jnp.abs(diff + (ks - qs)) <= n_side, 0.0, NEG_INF)
        bias2 = jnp.concatenate([bias, bias], axis=0)
        rows = slice(t * DIL_Q_TILE, (t + 1) * DIL_Q_TILE)
        for c in range(n_cls):
            for pr in range(DIL_GROUP_WIDTH // LANES):
                cols = slice(pr * LANES, (pr + 1) * LANES)
                qp = q_ref[c, rows, cols]
                kp = k_ref[c, pl.ds(ks, DIL_K_TILE), cols]
                vp = v_ref[c, pl.ds(ks, DIL_K_TILE), cols]
                zero = jnp.zeros_like(qp)
                q2 = jnp.concatenate([jnp.where(q_first, qp, zero), jnp.where(q_first, zero, qp)], axis=0)
                s = _dot_nt(q2, kp) + bias2
                m = jnp.max(s, axis=-1, keepdims=True)
                p = jnp.exp2(s - m).astype(BF16)
                pv = _dot(p, jnp.concatenate([vp, ones_blk], axis=1))
                pick = lambda a: jnp.where(low, a[:DIL_Q_TILE], a[DIL_Q_TILE:])
                num, den = pick(pv[:, :LANES]), pick(pv[:, LANES:])
                o_ref[c, rows, cols] = (num * (1.0 / den)).astype(BF16)
                lse_ref[c, rows, cols] = pick(jnp.broadcast_to(m, (2 * DIL_Q_TILE, LANES))) + jnp.log2(den)


def _dil_attn(q, k, v, n_side, *, subtiles_per_step=8):
    b, dil, length, w = q.shape
    assert w == DIL_GROUP_WIDTH and n_side == DIL_Q_TILE // 2
    assert DIL_K_TILE == DIL_Q_TILE + 2 * n_side and length >= DIL_K_TILE
    tq = min(length, subtiles_per_step * DIL_Q_TILE)
    n_cls = min(dil, (subtiles_per_step * DIL_Q_TILE) // tq)
    assert length % tq == 0 and tq % DIL_Q_TILE == 0 and dil % n_cls == 0
    q_spec = pl.BlockSpec((None, n_cls, tq, w), lambda bi, r, i: (bi, r, i, 0))
    kv_spec = pl.BlockSpec((None, n_cls, length, w), lambda bi, r, i: (bi, r, 0, 0))
    return pl.pallas_call(
        functools.partial(_dil_attn_kernel, n_side=n_side),
        grid=(b, dil // n_cls, length // tq),
        in_specs=[q_spec, kv_spec, kv_spec],
        out_specs=[q_spec, q_spec],
        out_shape=[jax.ShapeDtypeStruct(q.shape, BF16), jax.ShapeDtypeStruct(q.shape, F32)],
        compiler_params=_params(3),
        name=f"dil_attn_d{dil}",
    )(q, k, v)


def _mem_kv_kernel(mem_ref, g_ref, w_ref, k_ref, v_ref):
    hm = _rms(mem_ref[...], g_ref[...]).astype(BF16)
    kv = _dot(hm, w_ref[...])
    k_ref[...] = kv[:, :MEM_WIDTH].astype(BF16)
    v_ref[...] = kv[:, MEM_WIDTH:].astype(BF16)


def _mem_kv(mem, g, w_kv):
    b, m, d = mem.shape
    spec_out = pl.BlockSpec((None, m, MEM_WIDTH), lambda bi: (bi, 0, 0))
    return pl.pallas_call(
        _mem_kv_kernel,
        grid=(b,),
        in_specs=[pl.BlockSpec((None, m, d), lambda bi: (bi, 0, 0)),
                  _const_spec(g.shape), _const_spec(w_kv.shape)],
        out_specs=[spec_out, spec_out],
        out_shape=[jax.ShapeDtypeStruct((b, m, MEM_WIDTH), BF16)] * 2,
        compiler_params=_params(1),
        name="mem_kv",
    )(mem, g, w_kv)


def _mix_kernel(x_ref, g_ref, wgate_ref, omla_ref, womla_ref,
                o0_ref, l0_ref, o1_ref, l1_ref, o2_ref, l2_ref, wodil_ref,
                mq_ref, mk_ref, mv_ref, womem_ref, wout_ref,
                out_ref, so1_ref, sl1_ref, so2_ref, sl2_ref):
    x = x_ref[...]
    d = x.shape[1]
    h = _rms(x, g_ref[...]).astype(BF16)

    o_mem = []
    for hd in range(MEM_HEADS):
        blk = slice(hd * MEM_HEAD_DIM, (hd + 1) * MEM_HEAD_DIM)
        s = _dot_nt(mq_ref[:, blk], mk_ref[:, blk])
        m = jnp.max(s, axis=-1, keepdims=True)
        p = jnp.exp(s - m)
        l = jnp.sum(p, axis=-1, keepdims=True)
        o_mem.append((_dot(p.astype(BF16), mv_ref[:, blk]) * (1.0 / l)).astype(BF16))
    o_mem = jnp.concatenate(o_mem, axis=1)

    gblk = DIL_GROUP_WIDTH // LANES
    for o_ref, l_ref, so_ref, sl_ref in ((o1_ref, l1_ref, so1_ref, sl1_ref),
                                         (o2_ref, l2_ref, so2_ref, sl2_ref)):
        dil, n = o_ref.shape[0], o_ref.shape[1]
        for r in range(dil):
            rows = pl.ds(r, n, stride=dil)
            for jj in range(gblk):
                cols = slice(jj * LANES, (jj + 1) * LANES)
                so_ref[jj, rows, :] = o_ref[r, :, cols].astype(F32)
                sl_ref[jj, rows, :] = l_ref[r, :, cols]
    o_parts = []
    for jj in range(gblk):
        cols = slice(jj * LANES, (jj + 1) * LANES)
        l0, l1, l2 = l0_ref[:, cols], sl1_ref[jj], sl2_ref[jj]
        lmax = jnp.maximum(jnp.maximum(l0, l1), l2)
        w0, w1, w2 = jnp.exp2(l0 - lmax), jnp.exp2(l1 - lmax), jnp.exp2(l2 - lmax)
        o_sum = w0 * o0_ref[:, cols].astype(F32) + w1 * so1_ref[jj] + w2 * so2_ref[jj]
        o_parts.append((o_sum * (1.0 / (w0 + w1 + w2))).astype(BF16))
    o_dil = jnp.concatenate(o_parts, axis=1)
    o_mla = omla_ref[...]

    mixed = []
    for c0 in range(0, d, MXU_COLS):
        cols = slice(c0, c0 + MXU_COLS)
        branches = (_dot(o_mla, womla_ref[:, cols]), _dot(o_dil, wodil_ref[:, cols]),
                    _dot(o_mem, womem_ref[:, cols]))
        acc = None
        for br, y in enumerate(branches):
            gate = jax.nn.sigmoid(_dot(h, wgate_ref[:, br * d + c0:br * d + c0 + MXU_COLS]))
            acc = gate * y if acc is None else acc + gate * y
        mixed.append(acc.astype(BF16))
    out_ref[...] = x + _dot(jnp.concatenate(mixed, axis=1), wout_ref[...])


def _mix(x1, g, w_gate, o_mla, wo_mla, dil_outs, wo_dil, mq, mem_k, mem_v, wo_mem, w_out, *, tm=512):
    b, s, d = x1.shape
    assert s % tm == 0
    n_mem = mem_k.shape[1]
    tok = lambda width: pl.BlockSpec((None, tm, width), lambda bi, i: (bi, i, 0))
    in_specs = [tok(d), _const_spec(g.shape), _const_spec(w_gate.shape),
                tok(o_mla.shape[2]), _const_spec(wo_mla.shape)]
    args = [x1, g, w_gate, o_mla, wo_mla]
    for (o_g, lse_g), (_, dil) in zip(dil_outs, DIL_GROUPS):
        if dil == 1:
            spec = tok(DIL_GROUP_WIDTH)
            o_g, lse_g = o_g.reshape(b, s, DIL_GROUP_WIDTH), lse_g.reshape(b, s, DIL_GROUP_WIDTH)
        else:
            spec = pl.BlockSpec((None, dil, tm // dil, DIL_GROUP_WIDTH), lambda bi, i: (bi, 0, i, 0))
        in_specs += [spec, spec]
        args += [o_g, lse_g]
    in_specs += [_const_spec(wo_dil.shape), tok(MEM_WIDTH),
                 pl.BlockSpec((None, n_mem, MEM_WIDTH), lambda bi, i: (bi, 0, 0)),
                 pl.BlockSpec((None, n_mem, MEM_WIDTH), lambda bi, i: (bi, 0, 0)),
                 _const_spec(wo_mem.shape), _const_spec(w_out.shape)]
    args += [wo_dil, mq, mem_k, mem_v, wo_mem, w_out]
    return pl.pallas_call(
        _mix_kernel,
        grid=(b, s // tm),
        in_specs=in_specs,
        out_specs=tok(d),
        out_shape=jax.ShapeDtypeStruct((b, s, d), F32),
        scratch_shapes=[pltpu.VMEM((DIL_GROUP_WIDTH // LANES, tm, LANES), F32)] * 4,
        compiler_params=_params(2),
        name="mix",
    )(*args)


def _mla_lane_map():
    half = MLA_ROPE_DIM // 2
    r1 = MLA_ROPE_LANE
    lanes = np.full(LANES, MLA_QK_DIM)
    lanes[r1:r1 + half] = MLA_NOPE_DIM + np.arange(half)
    lanes[r1 + ROPE_PARTNER_SHIFT:r1 + ROPE_PARTNER_SHIFT + half] = MLA_NOPE_DIM + half + np.arange(half)
    free = [i for i in range(MLA_QK_DIM) if lanes[i] == MLA_QK_DIM]
    lanes[free] = np.arange(MLA_NOPE_DIM)
    return lanes


def _dil_lane_map():
    half = DIL_ROPE_DIM // 2
    a, b = np.arange(DIL_HEAD_DIM), DIL_HEAD_DIM + np.arange(DIL_HEAD_DIM)
    return np.concatenate([a[:half], b[:half], a[2 * half:], a[half:2 * half], b[half:2 * half], b[2 * half:]])


def _rope_consts():
    half_m, half_d = MLA_ROPE_DIM // 2, DIL_ROPE_DIM // 2
    f_m = ROPE_THETA ** (-2.0 * jnp.arange(half_m, dtype=F32) / MLA_ROPE_DIM)
    f_d = ROPE_THETA ** (-2.0 * jnp.arange(half_d, dtype=F32) / DIL_ROPE_DIM)
    rows = jnp.zeros((8, LANES), F32)
    for base, sign in ((0, -1.0), (ROPE_PARTNER_SHIFT, 1.0)):
        m0 = base + MLA_ROPE_LANE
        rows = rows.at[0, m0:m0 + half_m].set(f_m).at[1, m0:m0 + half_m].set(sign)
        rows = rows.at[0, base:base + half_d].set(f_d).at[0, base + half_d:base + 2 * half_d].set(f_d)
        rows = rows.at[2, base:base + 2 * half_d].set(sign)
    return rows


def _take_cols(w, idx):
    idx = np.asarray(idx)
    zero = idx < 0
    same_run = np.where(zero[1:] | zero[:-1], zero[1:] & zero[:-1], np.diff(idx) == 1)
    cuts = np.flatnonzero(~same_run) + 1
    parts = []
    for run in np.split(idx, cuts):
        if run[0] < 0:
            parts.append(jnp.zeros((w.shape[0], len(run)), w.dtype))
        else:
            parts.append(w[:, int(run[0]):int(run[-1]) + 1])
    return jnp.concatenate(parts, axis=1).astype(BF16)


def _prep_mix_weights(w_in, w_uq, w_ukv):
    n_lat = MLA_Q_RANK + MLA_KV_RANK
    kr0 = n_lat
    dil0 = kr0 + MLA_ROPE_DIM
    mq0 = dil0 + 3 * DIL_WIDTH
    gate0 = mq0 + MEM_WIDTH
    mla_map = _mla_lane_map()
    head = np.arange(MLA_HEADS)[:, None]
    kr_map = np.where((mla_map >= MLA_NOPE_DIM) & (mla_map < MLA_QK_DIM), kr0 + mla_map - MLA_NOPE_DIM, -1)
    n_pairs = 2 * DIL_WIDTH // LANES
    qk_map = dil0 + (np.arange(n_pairs)[:, None] * LANES + _dil_lane_map()[None, :]).reshape(-1)
    w_b = _take_cols(w_in, np.concatenate([np.arange(n_lat), kr_map, qk_map,
                                           np.arange(dil0 + 2 * DIL_WIDTH, gate0)]))
    w_gate = w_in[:, gate0:].astype(BF16)
    uq_map = np.where(mla_map < MLA_QK_DIM, head * MLA_QK_DIM + mla_map, -1).reshape(-1)
    wuq = _take_cols(w_uq, uq_map)
    kv_w = MLA_NOPE_DIM + MLA_V_DIM
    k_map = np.where(mla_map < MLA_NOPE_DIM, head * kv_w + mla_map, -1).reshape(-1)
    v_lane = np.arange(MLA_HEAD_PAD)
    v_map = np.where(v_lane < MLA_V_DIM, head * kv_w + MLA_NOPE_DIM + v_lane, -1).reshape(-1)
    wukv = _take_cols(w_ukv, np.concatenate([k_map, v_map]))
    return w_b, w_gate, wuq, wukv


def kernel(x, mem, positions, ffn1_norm, ffn1_w_gate, ffn1_w_up, ffn1_w_down, mix_norm, w_in, mla_q_norm, mla_w_uq, mla_kv_norm, mla_w_ukv, mla_w_o, dil_w_o, mem_norm, mem_w_kv, mem_w_o, w_out, ffn2_norm, ffn2_w_gate, ffn2_w_up, ffn2_w_down, final_norm):
    b, s, d = x.shape
    depth = ffn1_norm.shape[0]
    pos = positions.astype(F32).reshape(b, s, 1)
    rope_consts = _rope_consts()
    row = lambda v: v.reshape(1, -1)
    bf = lambda w: w.astype(BF16)
    for l in range(depth):
        x = _ffn(x.reshape(b * s, d), row(ffn1_norm[l]), bf(ffn1_w_gate[l]), bf(ffn1_w_up[l]),
                 bf(ffn1_w_down[l])).reshape(b, s, d)

        w_b, w_gate, wuq, wukv = _prep_mix_weights(w_in[l], mla_w_uq[l], mla_w_ukv[l])
        outs = _mixproj(x, pos, row(mix_norm[l]), w_b, row(mla_q_norm[l]), wuq,
                        row(mla_kv_norm[l]), wukv, rope_consts)
        q_m, k_m, v_m = outs[0:3]
        mq = outs[12]
        o_mla = _mla_attn(q_m, k_m, v_m)
        dil_outs = []
        for g, (window, dil) in enumerate(DIL_GROUPS):
            qg, kg, vg = outs[3 + 3 * g:6 + 3 * g]
            dil_outs.append(_dil_attn(qg, kg, vg, window // (2 * dil)))
        mem_k, mem_v = _mem_kv(mem, row(mem_norm[l]), bf(mem_w_kv[l]))
        x = _mix(x, row(mix_norm[l]), w_gate, o_mla, bf(mla_w_o[l]), dil_outs, bf(dil_w_o[l]),
                 mq, mem_k, mem_v, bf(mem_w_o[l]), bf(w_out[l]))

        last = l == depth - 1
        x = _ffn(x.reshape(b * s, d), row(ffn2_norm[l]), bf(ffn2_w_gate[l]), bf(ffn2_w_up[l]),
                 bf(ffn2_w_down[l]), row(final_norm) if last else None).reshape(b, s, d)
    return x
```

```python
import functools

import numpy as np
import jax
import jax.numpy as jnp
from jax import lax
from jax.experimental import pallas as pl
from jax.experimental.pallas import tpu as pltpu

F32 = jnp.float32
BF16 = jnp.bfloat16

NORM_EPS = 1e-6
NEG_INF = -1e30
ROPE_THETA = 500000.0
LOG2_E = 1.4426950408889634

LANES = 128
MXU_COLS = 256
VMEM_LIMIT_BYTES = 56 * 1024 * 1024

MLA_HEADS = 8
MLA_Q_RANK = 384
MLA_KV_RANK = 128
MLA_NOPE_DIM = 64
MLA_ROPE_DIM = 32
MLA_V_DIM = 64
MLA_QK_DIM = MLA_NOPE_DIM + MLA_ROPE_DIM
MLA_HEAD_PAD = 128
MLA_ROPE_LANE = 16
ROPE_PARTNER_SHIFT = LANES // 2

DIL_GROUPS = ((128, 1), (512, 4), (2048, 16))
DIL_HEADS = 4
DIL_HEAD_DIM = 64
DIL_ROPE_DIM = 16
DIL_GROUP_WIDTH = DIL_HEADS * DIL_HEAD_DIM
DIL_WIDTH = len(DIL_GROUPS) * DIL_GROUP_WIDTH
DIL_Q_TILE = 128
DIL_K_TILE = 256

MEM_HEADS = 4
MEM_HEAD_DIM = 128
MEM_WIDTH = MEM_HEADS * MEM_HEAD_DIM

N_BRANCHES = 3


def _dot(a, b):
    return jnp.dot(a, b, preferred_element_type=F32)


def _dot_nt(a, b):
    return lax.dot_general(a, b, (((1,), (1,)), ((), ())), preferred_element_type=F32)


def _rms(x, g):
    y = x * lax.rsqrt(jnp.mean(x * x, axis=-1, keepdims=True) + NORM_EPS)
    return y * g


def _const_spec(shape):
    nd = len(shape)
    return pl.BlockSpec(shape, lambda *_: (0,) * nd, pipeline_mode=pl.Buffered(1))


def _params(n_grid):
    return pltpu.CompilerParams(
        dimension_semantics=("arbitrary",) * n_grid,
        vmem_limit_bytes=VMEM_LIMIT_BYTES,
    )


def _ffn_kernel(x_ref, g_ref, wg_ref, wu_ref, wd_ref, *rest, f_chunk, final):
    if final:
        fg_ref, o_ref = rest
    else:
        (o_ref,) = rest
    x = x_ref[...]
    h = _rms(x, g_ref[...]).astype(BF16)
    d_ff = wg_ref.shape[1]
    acc = jnp.zeros(x.shape, F32)
    for c in range(d_ff // f_chunk):
        cols = slice(c * f_chunk, (c + 1) * f_chunk)
        a = _dot(h, wg_ref[:, cols])
        b = _dot(h, wu_ref[:, cols])
        act = (jax.nn.silu(a) * b).astype(BF16)
        acc = acc + _dot(act, wd_ref[cols, :])
    y = x + 0.5 * acc
    if final:
        y = _rms(y, fg_ref[...])
    o_ref[...] = y


def _ffn(x2d, g, wg, wu, wd, final_g=None, *, tm=1024, f_chunk=256):
    t, d = x2d.shape
    d_ff = wg.shape[1]
    assert t % tm == 0 and d_ff % f_chunk == 0
    final = final_g is not None
    in_specs = [
        pl.BlockSpec((tm, d), lambda i: (i, 0)),
        _const_spec((1, d)),
        _const_spec((d, d_ff)),
        _const_spec((d, d_ff)),
        _const_spec((d_ff, d)),
    ]
    args = [x2d, g, wg, wu, wd]
    if final:
        in_specs.append(_const_spec((1, d)))
        args.append(final_g)
    return pl.pallas_call(
        functools.partial(_ffn_kernel, f_chunk=f_chunk, final=final),
        grid=(t // tm,),
        in_specs=in_specs,
        out_specs=pl.BlockSpec((tm, d), lambda i: (i, 0)),
        out_shape=jax.ShapeDtypeStruct((t, d), F32),
        compiler_params=_params(1),
        name="ffn_final" if final else "ffn",
    )(*args)


def _rope_apply(x, c, s):
    return x * c + pltpu.roll(x, ROPE_PARTNER_SHIFT, axis=1) * s


def _mixproj_kernel(x_ref, pos_ref, g_ref, w_ref, qn_ref, wuq_ref, kvn_ref, wukv_ref, rc_ref,
                    qm_ref, km_ref, vm_ref,
                    q0_ref, k0_ref, v0_ref, q1_ref, k1_ref, v1_ref, q2_ref, k2_ref, v2_ref,
                    mq_ref, zd_ref):
    tm = x_ref.shape[0]
    h = _rms(x_ref[...], g_ref[...]).astype(BF16)
    pos = pos_ref[...]
    rc = rc_ref[...]

    ang = pos * rc[0:1]
    cos_t, sin_t = jnp.cos(ang), jnp.sin(ang)
    sgn_m, sgn_d = rc[1:2], rc[2:3]
    c_m, s_m = jnp.where(sgn_d != 0.0, 1.0, cos_t), sin_t * sgn_m
    c_d, s_d = jnp.where(sgn_m != 0.0, 1.0, cos_t), sin_t * sgn_d

    q_scale = MLA_QK_DIM ** -0.5 * LOG2_E
    cq_m, sq_m = c_m * q_scale, s_m * q_scale

    z = _dot(h, w_ref[:, 0:640])
    cq = _rms(z[:, 0:MLA_Q_RANK], qn_ref[...]).astype(BF16)
    ckv = _rms(z[:, MLA_Q_RANK:MLA_Q_RANK + MLA_KV_RANK], kvn_ref[...]).astype(BF16)
    kr = _rope_apply(z[:, 512:640], c_m, s_m)

    mq0 = 640 + 3 * DIL_WIDTH
    for c0 in range(0, MEM_WIDTH, MXU_COLS):
        mq_c = _dot(h, w_ref[:, mq0 + c0:mq0 + c0 + MXU_COLS])
        mq_ref[:, c0:c0 + MXU_COLS] = (mq_c * (MEM_HEAD_DIM ** -0.5)).astype(BF16)

    d_scale = DIL_HEAD_DIM ** -0.5 * LOG2_E
    cq_d, sq_d = c_d * d_scale, s_d * d_scale
    nblk = DIL_WIDTH // LANES
    for c0 in reversed(range(0, 3 * DIL_WIDTH, MXU_COLS)):
        zd_c = _dot(h, w_ref[:, 640 + c0:640 + c0 + MXU_COLS])
        for o in range(0, MXU_COLS, LANES):
            j = (c0 + o) // LANES
            part = zd_c[:, o:o + LANES]
            if j < nblk:
                zd_ref[j] = _rope_apply(part, cq_d, sq_d)
            elif j < 2 * nblk:
                zd_ref[j] = _rope_apply(part, c_d, s_d)
            else:
                zd_ref[j] = part

    outs = ((q0_ref, k0_ref, v0_ref), (q1_ref, k1_ref, v1_ref), (q2_ref, k2_ref, v2_ref))
    gblk = DIL_GROUP_WIDTH // LANES
    for g, (_, dil) in enumerate(DIL_GROUPS):
        for part in range(3):
            o_ref = outs[g][part]
            for jj in range(gblk):
                j = part * nblk + g * gblk + jj
                cols = slice(jj * LANES, (jj + 1) * LANES)
                if dil == 1:
                    o_ref[0, :, cols] = zd_ref[j].astype(BF16)
                else:
                    for r in range(dil):
                        o_ref[r, :, cols] = zd_ref[j, pl.ds(r, tm // dil, stride=dil), :].astype(BF16)


    lane = lax.broadcasted_iota(jnp.int32, (1, LANES), 1)
    ones_col = (lane == MLA_V_DIM).astype(F32)
    hw = MLA_HEADS * MLA_HEAD_PAD
    for c0 in range(0, hw, MXU_COLS):
        q_c = _dot(cq, wuq_ref[:, c0:c0 + MXU_COLS])
        k_c = _dot(ckv, wukv_ref[:, c0:c0 + MXU_COLS])
        v_c = _dot(ckv, wukv_ref[:, hw + c0:hw + c0 + MXU_COLS])
        for o in range(0, MXU_COLS, MLA_HEAD_PAD):
            src, dst = slice(o, o + MLA_HEAD_PAD), slice(c0 + o, c0 + o + MLA_HEAD_PAD)
            qm_ref[:, dst] = _rope_apply(q_c[:, src], cq_m, sq_m).astype(BF16)
            km_ref[:, dst] = (k_c[:, src] + kr).astype(BF16)
            vm_ref[:, dst] = (v_c[:, src] + ones_col).astype(BF16)


def _mixproj(x1, pos, g, w_b, qn, wuq, kvn, wukv, rope_consts, *, tm=512):
    b, s, d = x1.shape
    assert s % tm == 0 and tm % 16 == 0
    hw = MLA_HEADS * MLA_HEAD_PAD
    tok = lambda width: pl.BlockSpec((None, tm, width), lambda bi, i: (bi, i, 0))
    in_specs = [
        tok(d),
        tok(1),
        _const_spec(g.shape),
        _const_spec(w_b.shape),
        _const_spec(qn.shape),
        _const_spec(wuq.shape),
        _const_spec(kvn.shape),
        _const_spec(wukv.shape),
        _const_spec(rope_consts.shape),
    ]
    out_shapes = [jax.ShapeDtypeStruct((b, s, hw), BF16)] * 3
    out_specs = [tok(hw)] * 3
    for _, dil in DIL_GROUPS:
        for _ in range(3):
            out_shapes.append(jax.ShapeDtypeStruct((b, dil, s // dil, DIL_GROUP_WIDTH), BF16))
            out_specs.append(pl.BlockSpec((None, dil, tm // dil, DIL_GROUP_WIDTH),
                                          lambda bi, i: (bi, 0, i, 0)))
    out_shapes.append(jax.ShapeDtypeStruct((b, s, MEM_WIDTH), BF16))
    out_specs.append(tok(MEM_WIDTH))
    return pl.pallas_call(
        _mixproj_kernel,
        grid=(b, s // tm),
        in_specs=in_specs,
        out_specs=out_specs,
        out_shape=out_shapes,
        scratch_shapes=[pltpu.VMEM((3 * DIL_WIDTH // LANES, tm, LANES), F32)],
        compiler_params=_params(2),
        name="mixproj",
    )(x1, pos, g, w_b, qn, wuq, kvn, wukv, rope_consts)


def _mla_attn_kernel(q_ref, k_ref, v_ref, o_ref, *, tk, unroll):
    tq = q_ref.shape[0]
    s_len = k_ref.shape[0]
    heads = q_ref.shape[1] // MLA_HEAD_PAD
    blks = [slice(hd * MLA_HEAD_PAD, (hd + 1) * MLA_HEAD_PAD) for hd in range(heads)]

    def body(j, carry):
        rows = pl.ds(pl.multiple_of(j * tk, tk), tk)
        new = []
        for blk, (m, acc) in zip(blks, carry):
            s = _dot_nt(q_ref[:, blk], k_ref[rows, blk])
            m_new = jnp.maximum(m, jnp.max(s, axis=-1, keepdims=True))
            p = jnp.exp2(s - m_new).astype(BF16)
            acc = jnp.exp2(m - m_new) * acc + _dot(p, v_ref[rows, blk])
            new.append((m_new, acc))
        return tuple(new)

    init = tuple((jnp.full((tq, 1), NEG_INF, F32), jnp.zeros((tq, MLA_HEAD_PAD), F32))
                 for _ in range(heads))
    final = lax.fori_loop(0, s_len // tk, body, init, unroll=unroll)
    outs = [acc * (1.0 / acc[:, MLA_V_DIM:MLA_V_DIM + 1]) for _, acc in final]
    lane = lax.broadcasted_iota(jnp.int32, (1, LANES), 1)
    for pr in range(heads // 2):
        even, odd = outs[2 * pr], outs[2 * pr + 1]
        pair = jnp.where(lane < MLA_V_DIM, even, pltpu.roll(odd, MLA_V_DIM, axis=1))
        o_ref[:, pr * LANES:(pr + 1) * LANES] = pair.astype(BF16)


def _mla_attn(q, k, v, *, tq=1024, tk=2048, heads_per_step=4, unroll=2):
    b, s, hw = q.shape
    assert s % tq == 0 and s % tk == 0 and heads_per_step % 2 == 0
    wblk = heads_per_step * MLA_HEAD_PAD
    oblk = heads_per_step * MLA_V_DIM
    return pl.pallas_call(
        functools.partial(_mla_attn_kernel, tk=tk, unroll=unroll),
        grid=(b, hw // wblk, s // tq),
        in_specs=[
            pl.BlockSpec((None, tq, wblk), lambda bi, hi, i: (bi, i, hi)),
            pl.BlockSpec((None, s, wblk), lambda bi, hi, i: (bi, 0, hi)),
            pl.BlockSpec((None, s, wblk), lambda bi, hi, i: (bi, 0, hi)),
        ],
        out_specs=pl.BlockSpec((None, tq, oblk), lambda bi, hi, i: (bi, i, hi)),
        out_shape=jax.ShapeDtypeStruct((b, s, MLA_HEADS * MLA_V_DIM), BF16),
        compiler_params=_params(3),
        name="mla_attn",
    )(q, k, v)


def _lane_mask(lane, flags):
    edges = np.flatnonzero(np.diff(np.concatenate([[0], flags.astype(np.int8), [0]])))
    mask = None
    for lo, hi in zip(edges[0::2], edges[1::2]):
        run = (lane >= int(lo)) & (lane < int(hi))
        mask = run if mask is None else (mask | run)
    return mask


def _dil_attn_kernel(q_ref, k_ref, v_ref, o_ref, lse_ref, *, n_side):
    n_cls, tq_all, _ = q_ref.shape
    length = k_ref.shape[1]
    i = pl.program_id(2)
    lane = lax.broadcasted_iota(jnp.int32, (1, LANES), 1)
    low = lane < DIL_HEAD_DIM
    q_first = _lane_mask(lane, _dil_lane_map() < DIL_HEAD_DIM)
    diff = (lax.broadcasted_iota(jnp.int32, (DIL_Q_TILE, DIL_K_TILE), 1)
            - lax.broadcasted_iota(jnp.int32, (DIL_Q_TILE, DIL_K_TILE), 0))
    ones_blk = jnp.ones((DIL_K_TILE, LANES), BF16)
    for t in range(tq_all // DIL_Q_TILE):
        qs = i * tq_all + t * DIL_Q_TILE
        ks = pl.multiple_of(jnp.clip(qs - n_side, 0, length - DIL_K_TILE), n_side)
        bias = jnp.where(jnp.abs(diff + (ks - qs)) <= n_side, 0.0, NEG_INF)
        bias2 = jnp.concatenate([bias, bias], axis=0)
        rows = slice(t * DIL_Q_TILE, (t + 1) * DIL_Q_TILE)
        for c in range(n_cls):
            for pr in range(DIL_GROUP_WIDTH // LANES):
                cols = slice(pr * LANES, (pr + 1) * LANES)
                qp = q_ref[c, rows, cols]
                kp = k_ref[c, pl.ds(ks, DIL_K_TILE), cols]
                vp = v_ref[c, pl.ds(ks, DIL_K_TILE), cols]
                zero = jnp.zeros_like(qp)
                q2 = jnp.concatenate([jnp.where(q_first, qp, zero), jnp.where(q_first, zero, qp)], axis=0)
                s = _dot_nt(q2, kp) + bias2
                m = jnp.max(s, axis=-1, keepdims=True)
                p = jnp.exp2(s - m).astype(BF16)
                pv = _dot(p, jnp.concatenate([vp, ones_blk], axis=1))
                pick = lambda a: jnp.where(low, a[:DIL_Q_TILE], a[DIL_Q_TILE:])
                num, den = pick(pv[:, :LANES]), pick(pv[:, LANES:])
                o_ref[c, rows, cols] = (num * (1.0 / den)).astype(BF16)
                lse_ref[c, rows, cols] = pick(jnp.broadcast_to(m, (2 * DIL_Q_TILE, LANES))) + jnp.log2(den)


def _dil_attn(q, k, v, n_side, *, subtiles_per_step=8):
    b, dil, length, w = q.shape
    assert w == DIL_GROUP_WIDTH and n_side == DIL_Q_TILE // 2
    assert DIL_K_TILE == DIL_Q_TILE + 2 * n_side and length >= DIL_K_TILE
    tq = min(length, subtiles_per_step * DIL_Q_TILE)
    n_cls = min(dil, (subtiles_per_step * DIL_Q_TILE) // tq)
    assert length % tq == 0 and tq % DIL_Q_TILE == 0 and dil % n_cls == 0
    q_spec = pl.BlockSpec((None, n_cls, tq, w), lambda bi, r, i: (bi, r, i, 0))
    kv_spec = pl.BlockSpec((None, n_cls, length, w), lambda bi, r, i: (bi, r, 0, 0))
    return pl.pallas_call(
        functools.partial(_dil_attn_kernel, n_side=n_side),
        grid=(b, dil // n_cls, length // tq),
        in_specs=[q_spec, kv_spec, kv_spec],
        out_specs=[q_spec, q_spec],
        out_shape=[jax.ShapeDtypeStruct(q.shape, BF16), jax.ShapeDtypeStruct(q.shape, F32)],
        compiler_params=_params(3),
        name=f"dil_attn_d{dil}",
    )(q, k, v)


def _mem_kv_kernel(mem_ref, g_ref, w_ref, k_ref, v_ref):
    hm = _rms(mem_ref[...], g_ref[...]).astype(BF16)
    kv = _dot(hm, w_ref[...])
    k_ref[...] = kv[:, :MEM_WIDTH].astype(BF16)
    v_ref[...] = kv[:, MEM_WIDTH:].astype(BF16)


def _mem_kv(mem, g, w_kv):
    b, m, d = mem.shape
    spec_out = pl.BlockSpec((None, m, MEM_WIDTH), lambda bi: (bi, 0, 0))
    return pl.pallas_call(
        _mem_kv_kernel,
        grid=(b,),
        in_specs=[pl.BlockSpec((None, m, d), lambda bi: (bi, 0, 0)),
                  _const_spec(g.shape), _const_spec(w_kv.shape)],
        out_specs=[spec_out, spec_out],
        out_shape=[jax.ShapeDtypeStruct((b, m, MEM_WIDTH), BF16)] * 2,
        compiler_params=_params(1),
        name="mem_kv",
    )(mem, g, w_kv)


def _mix_kernel(x_ref, g_ref, wgate_ref, omla_ref, womla_ref,
                o0_ref, l0_ref, o1_ref, l1_ref, o2_ref, l2_ref, wodil_ref,
                mq_ref, mk_ref, mv_ref, womem_ref, wout_ref,
                out_ref, so1_ref, sl1_ref, so2_ref, sl2_ref):
    x = x_ref[...]
    d = x.shape[1]
    h = _rms(x, g_ref[...]).astype(BF16)

    o_mla = omla_ref[...]
    head_blks = [slice(hd * MEM_HEAD_DIM, (hd + 1) * MEM_HEAD_DIM) for hd in range(MEM_HEADS)]
    scores = [_dot_nt(mq_ref[:, blk], mk_ref[:, blk]) for blk in head_blks]

    col_chunks = [slice(c0, c0 + MXU_COLS) for c0 in range(0, d, MXU_COLS)]

    def gate(branch, cols):
        lo = branch * d + cols.start
        return jax.nn.sigmoid(_dot(h, wgate_ref[:, lo:lo + MXU_COLS]))

    mixed = [gate(0, cols) * _dot(o_mla, womla_ref[:, cols]) for cols in col_chunks]
    gates_dil = [gate(1, cols) for cols in col_chunks]
    gates_mem = [gate(2, cols) for cols in col_chunks]

    o_mem = []
    for blk, s in zip(head_blks, scores):
        m = jnp.max(s, axis=-1, keepdims=True)
        p = jnp.exp(s - m)
        l = jnp.sum(p, axis=-1, keepdims=True)
        o_mem.append((_dot(p.astype(BF16), mv_ref[:, blk]) * (1.0 / l)).astype(BF16))
    o_mem = jnp.concatenate(o_mem, axis=1)

    gblk = DIL_GROUP_WIDTH // LANES
    for o_ref, l_ref, so_ref, sl_ref in ((o1_ref, l1_ref, so1_ref, sl1_ref),
                                         (o2_ref, l2_ref, so2_ref, sl2_ref)):
        dil, n = o_ref.shape[0], o_ref.shape[1]
        for r in range(dil):
            rows = pl.ds(r, n, stride=dil)
            for jj in range(gblk):
                cols = slice(jj * LANES, (jj + 1) * LANES)
                so_ref[jj, rows, :] = o_ref[r, :, cols].astype(F32)
                sl_ref[jj, rows, :] = l_ref[r, :, cols]
    o_parts = []
    for jj in range(gblk):
        cols = slice(jj * LANES, (jj + 1) * LANES)
        l0, l1, l2 = l0_ref[:, cols], sl1_ref[jj], sl2_ref[jj]
        lmax = jnp.maximum(jnp.maximum(l0, l1), l2)
        w0, w1, w2 = jnp.exp2(l0 - lmax), jnp.exp2(l1 - lmax), jnp.exp2(l2 - lmax)
        o_sum = w0 * o0_ref[:, cols].astype(F32) + w1 * so1_ref[jj] + w2 * so2_ref[jj]
        o_parts.append((o_sum * (1.0 / (w0 + w1 + w2))).astype(BF16))
    o_dil = jnp.concatenate(o_parts, axis=1)

    for i, cols in enumerate(col_chunks):
        acc = mixed[i] + gates_dil[i] * _dot(o_dil, wodil_ref[:, cols])
        acc = acc + gates_mem[i] * _dot(o_mem, womem_ref[:, cols])
        mixed[i] = acc.astype(BF16)
    out_ref[...] = x + _dot(jnp.concatenate(mixed, axis=1), wout_ref[...])


def _mix(x1, g, w_gate, o_mla, wo_mla, dil_outs, wo_dil, mq, mem_k, mem_v, wo_mem, w_out, *, tm=512):
    b, s, d = x1.shape
    assert s % tm == 0
    n_mem = mem_k.shape[1]
    tok = lambda width: pl.BlockSpec((None, tm, width), lambda bi, i: (bi, i, 0))
    in_specs = [tok(d), _const_spec(g.shape), _const_spec(w_gate.shape),
                tok(o_mla.shape[2]), _const_spec(wo_mla.shape)]
    args = [x1, g, w_gate, o_mla, wo_mla]
    for (o_g, lse_g), (_, dil) in zip(dil_outs, DIL_GROUPS):
        if dil == 1:
            spec = tok(DIL_GROUP_WIDTH)
            o_g, lse_g = o_g.reshape(b, s, DIL_GROUP_WIDTH), lse_g.reshape(b, s, DIL_GROUP_WIDTH)
        else:
            spec = pl.BlockSpec((None, dil, tm // dil, DIL_GROUP_WIDTH), lambda bi, i: (bi, 0, i, 0))
        in_specs += [spec, spec]
        args += [o_g, lse_g]
    in_specs += [_const_spec(wo_dil.shape), tok(MEM_WIDTH),
                 pl.BlockSpec((None, n_mem, MEM_WIDTH), lambda bi, i: (bi, 0, 0)),
                 pl.BlockSpec((None, n_mem, MEM_WIDTH), lambda bi, i: (bi, 0, 0)),
                 _const_spec(wo_mem.shape), _const_spec(w_out.shape)]
    args += [wo_dil, mq, mem_k, mem_v, wo_mem, w_out]
    return pl.pallas_call(
        _mix_kernel,
        grid=(b, s // tm),
        in_specs=in_specs,
        out_specs=tok(d),
        out_shape=jax.ShapeDtypeStruct((b, s, d), F32),
        scratch_shapes=[pltpu.VMEM((DIL_GROUP_WIDTH // LANES, tm, LANES), F32)] * 4,
        compiler_params=_params(2),
        name="mix",
    )(*args)


def _mla_lane_map():
    half = MLA_ROPE_DIM // 2
    r1 = MLA_ROPE_LANE
    lanes = np.full(LANES, MLA_QK_DIM)
    lanes[r1:r1 + half] = MLA_NOPE_DIM + np.arange(half)
    lanes[r1 + ROPE_PARTNER_SHIFT:r1 + ROPE_PARTNER_SHIFT + half] = MLA_NOPE_DIM + half + np.arange(half)
    free = [i for i in range(MLA_QK_DIM) if lanes[i] == MLA_QK_DIM]
    lanes[free] = np.arange(MLA_NOPE_DIM)
    return lanes


def _dil_lane_map():
    half = DIL_ROPE_DIM // 2
    a, b = np.arange(DIL_HEAD_DIM), DIL_HEAD_DIM + np.arange(DIL_HEAD_DIM)
    return np.concatenate([a[:half], b[:half], a[2 * half:], a[half:2 * half], b[half:2 * half], b[2 * half:]])


def _rope_consts():
    half_m, half_d = MLA_ROPE_DIM // 2, DIL_ROPE_DIM // 2
    f_m = ROPE_THETA ** (-2.0 * jnp.arange(half_m, dtype=F32) / MLA_ROPE_DIM)
    f_d = ROPE_THETA ** (-2.0 * jnp.arange(half_d, dtype=F32) / DIL_ROPE_DIM)
    rows = jnp.zeros((8, LANES), F32)
    for base, sign in ((0, -1.0), (ROPE_PARTNER_SHIFT, 1.0)):
        m0 = base + MLA_ROPE_LANE
        rows = rows.at[0, m0:m0 + half_m].set(f_m).at[1, m0:m0 + half_m].set(sign)
        rows = rows.at[0, base:base + half_d].set(f_d).at[0, base + half_d:base + 2 * half_d].set(f_d)
        rows = rows.at[2, base:base + 2 * half_d].set(sign)
    return rows


def _take_cols(w, idx):
    idx = np.asarray(idx)
    zero = idx < 0
    same_run = np.where(zero[1:] | zero[:-1], zero[1:] & zero[:-1], np.diff(idx) == 1)
    cuts = np.flatnonzero(~same_run) + 1
    parts = []
    for run in np.split(idx, cuts):
        if run[0] < 0:
            parts.append(jnp.zeros((w.shape[0], len(run)), w.dtype))
        else:
            parts.append(w[:, int(run[0]):int(run[-1]) + 1])
    return jnp.concatenate(parts, axis=1).astype(BF16)


def _prep_mix_weights(w_in, w_uq, w_ukv):
    n_lat = MLA_Q_RANK + MLA_KV_RANK
    kr0 = n_lat
    dil0 = kr0 + MLA_ROPE_DIM
    mq0 = dil0 + 3 * DIL_WIDTH
    gate0 = mq0 + MEM_WIDTH
    mla_map = _mla_lane_map()
    head = np.arange(MLA_HEADS)[:, None]
    kr_map = np.where((mla_map >= MLA_NOPE_DIM) & (mla_map < MLA_QK_DIM), kr0 + mla_map - MLA_NOPE_DIM, -1)
    n_pairs = 2 * DIL_WIDTH // LANES
    qk_map = dil0 + (np.arange(n_pairs)[:, None] * LANES + _dil_lane_map()[None, :]).reshape(-1)
    w_b = _take_cols(w_in, np.concatenate([np.arange(n_lat), kr_map, qk_map,
                                           np.arange(dil0 + 2 * DIL_WIDTH, gate0)]))
    w_gate = w_in[:, gate0:].astype(BF16)
    uq_map = np.where(mla_map < MLA_QK_DIM, head * MLA_QK_DIM + mla_map, -1).reshape(-1)
    wuq = _take_cols(w_uq, uq_map)
    kv_w = MLA_NOPE_DIM + MLA_V_DIM
    k_map = np.where(mla_map < MLA_NOPE_DIM, head * kv_w + mla_map, -1).reshape(-1)
    v_lane = np.arange(MLA_HEAD_PAD)
    v_map = np.where(v_lane < MLA_V_DIM, head * kv_w + MLA_NOPE_DIM + v_lane, -1).reshape(-1)
    wukv = _take_cols(w_ukv, np.concatenate([k_map, v_map]))
    return w_b, w_gate, wuq, wukv


def kernel(x, mem, positions, ffn1_norm, ffn1_w_gate, ffn1_w_up, ffn1_w_down, mix_norm, w_in, mla_q_norm, mla_w_uq, mla_kv_norm, mla_w_ukv, mla_w_o, dil_w_o, mem_norm, mem_w_kv, mem_w_o, w_out, ffn2_norm, ffn2_w_gate, ffn2_w_up, ffn2_w_down, final_norm):
    b, s, d = x.shape
    depth = ffn1_norm.shape[0]
    pos = positions.astype(F32).reshape(b, s, 1)
    rope_consts = _rope_consts()
    row = lambda v: v.reshape(1, -1)
    bf = lambda w: w.astype(BF16)
    for l in range(depth):
        x = _ffn(x.reshape(b * s, d), row(ffn1_norm[l]), bf(ffn1_w_gate[l]), bf(ffn1_w_up[l]),
                 bf(ffn1_w_down[l])).reshape(b, s, d)

        w_b, w_gate, wuq, wukv = _prep_mix_weights(w_in[l], mla_w_uq[l], mla_w_ukv[l])
        outs = _mixproj(x, pos, row(mix_norm[l]), w_b, row(mla_q_norm[l]), wuq,
                        row(mla_kv_norm[l]), wukv, rope_consts)
        q_m, k_m, v_m = outs[0:3]
        mq = outs[12]
        o_mla = _mla_attn(q_m, k_m, v_m)
        dil_outs = []
        for g, (window, dil) in enumerate(DIL_GROUPS):
            qg, kg, vg = outs[3 + 3 * g:6 + 3 * g]
            dil_outs.append(_dil_attn(qg, kg, vg, window // (2 * dil)))
        mem_k, mem_v = _mem_kv(mem, row(mem_norm[l]), bf(mem_w_kv[l]))
        x = _mix(x, row(mix_norm[l]), w_gate, o_mla, bf(mla_w_o[l]), dil_outs, bf(dil_w_o[l]),
                 mq, mem_k, mem_v, bf(mem_w_o[l]), bf(w_out[l]))

        last = l == depth - 1
        x = _ffn(x.reshape(b * s, d), row(ffn2_norm[l]), bf(ffn2_w_gate[l]), bf(ffn2_w_up[l]),
                 bf(ffn2_w_down[l]), row(final_norm) if last else None).reshape(b, s, d)
    return x
```

```python
import functools

import numpy as np
import jax
import jax.numpy as jnp
from jax import lax
from jax.experimental import pallas as pl
from jax.experimental.pallas import tpu as pltpu

F32 = jnp.float32
BF16 = jnp.bfloat16

NORM_EPS = 1e-6
NEG_INF = -1e30
ROPE_THETA = 500000.0
LOG2_E = 1.4426950408889634

LANES = 128
MXU_COLS = 256
VMEM_LIMIT_BYTES = 56 * 1024 * 1024

MLA_HEADS = 8
MLA_Q_RANK = 384
MLA_KV_RANK = 128
MLA_NOPE_DIM = 64
MLA_ROPE_DIM = 32
MLA_V_DIM = 64
MLA_QK_DIM = MLA_NOPE_DIM + MLA_ROPE_DIM
MLA_HEAD_PAD = 128
MLA_ROPE_LANE = 16
ROPE_PARTNER_SHIFT = LANES // 2

DIL_GROUPS = ((128, 1), (512, 4), (2048, 16))
DIL_HEADS = 4
DIL_HEAD_DIM = 64
DIL_ROPE_DIM = 16
DIL_GROUP_WIDTH = DIL_HEADS * DIL_HEAD_DIM
DIL_WIDTH = len(DIL_GROUPS) * DIL_GROUP_WIDTH
DIL_Q_TILE = 128
DIL_K_TILE = 256

MEM_HEADS = 4
MEM_HEAD_DIM = 128
MEM_WIDTH = MEM_HEADS * MEM_HEAD_DIM

N_BRANCHES = 3


def _dot(a, b):
    return jnp.dot(a, b, preferred_element_type=F32)


def _dot_nt(a, b):
    return lax.dot_general(a, b, (((1,), (1,)), ((), ())), preferred_element_type=F32)


def _rms(x, g):
    y = x * lax.rsqrt(jnp.mean(x * x, axis=-1, keepdims=True) + NORM_EPS)
    return y * g


def _const_spec(shape):
    nd = len(shape)
    return pl.BlockSpec(shape, lambda *_: (0,) * nd, pipeline_mode=pl.Buffered(1))


def _params(n_grid):
    return pltpu.CompilerParams(
        dimension_semantics=("arbitrary",) * n_grid,
        vmem_limit_bytes=VMEM_LIMIT_BYTES,
    )


def _ffn_kernel(x_ref, g_ref, wg_ref, wu_ref, wd_ref, *rest, f_chunk, final):
    if final:
        fg_ref, o_ref = rest
    else:
        (o_ref,) = rest
    x = x_ref[...]
    h = _rms(x, g_ref[...]).astype(BF16)
    d_ff = wg_ref.shape[1]
    acc = jnp.zeros(x.shape, F32)
    for c in range(d_ff // f_chunk):
        cols = slice(c * f_chunk, (c + 1) * f_chunk)
        a = _dot(h, wg_ref[:, cols])
        b = _dot(h, wu_ref[:, cols])
        act = (jax.nn.silu(a) * b).astype(BF16)
        acc = acc + _dot(act, wd_ref[cols, :])
    y = x + 0.5 * acc
    if final:
        y = _rms(y, fg_ref[...])
    o_ref[...] = y


def _ffn(x2d, g, wg, wu, wd, final_g=None, *, tm=1024, f_chunk=256):
    t, d = x2d.shape
    d_ff = wg.shape[1]
    assert t % tm == 0 and d_ff % f_chunk == 0
    final = final_g is not None
    in_specs = [
        pl.BlockSpec((tm, d), lambda i: (i, 0)),
        _const_spec((1, d)),
        _const_spec((d, d_ff)),
        _const_spec((d, d_ff)),
        _const_spec((d_ff, d)),
    ]
    args = [x2d, g, wg, wu, wd]
    if final:
        in_specs.append(_const_spec((1, d)))
        args.append(final_g)
    return pl.pallas_call(
        functools.partial(_ffn_kernel, f_chunk=f_chunk, final=final),
        grid=(t // tm,),
        in_specs=in_specs,
        out_specs=pl.BlockSpec((tm, d), lambda i: (i, 0)),
        out_shape=jax.ShapeDtypeStruct((t, d), F32),
        compiler_params=_params(1),
        name="ffn_final" if final else "ffn",
    )(*args)


def _rope_apply(x, c, s):
    return x * c + pltpu.roll(x, ROPE_PARTNER_SHIFT, axis=1) * s


def _rope_apply_near(x, c, s_up, s_down):
    half = DIL_ROPE_DIM // 2
    return x * c + pltpu.roll(x, half, axis=1) * s_up + pltpu.roll(x, LANES - half, axis=1) * s_down


def _mixproj_kernel(x_ref, pos_ref, g_ref, w_ref, qn_ref, wuq_ref, kvn_ref, wukv_ref, rc_ref,
                    qm_ref, km_ref, vm_ref,
                    q0_ref, k0_ref, v0_ref, q1_ref, k1_ref, v1_ref, q2_ref, k2_ref, v2_ref,
                    mq_ref, zd_ref):
    tm = x_ref.shape[0]
    h = _rms(x_ref[...], g_ref[...]).astype(BF16)
    rows_per_tile = tm // LANES
    row0 = pl.program_id(1) * rows_per_tile
    pos = jnp.concatenate(
        [jnp.broadcast_to(pos_ref[pl.ds(row0 + r, 1), :], (LANES, LANES)).T for r in range(rows_per_tile)],
        axis=0)
    rc = rc_ref[...]

    ang = pos * rc[0:1]
    cos_t, sin_t = jnp.cos(ang), jnp.sin(ang)
    sgn_m, up_d, down_d = rc[1:2], rc[2:3], rc[3:4]
    c_m, s_m = jnp.where((up_d != 0.0) | (down_d != 0.0), 1.0, cos_t), sin_t * sgn_m
    c_d, su_d, sd_d = jnp.where(sgn_m != 0.0, 1.0, cos_t), sin_t * up_d, sin_t * down_d

    q_scale = MLA_QK_DIM ** -0.5 * LOG2_E
    cq_m, sq_m = c_m * q_scale, s_m * q_scale

    z = _dot(h, w_ref[:, 0:640])
    cq = _rms(z[:, 0:MLA_Q_RANK], qn_ref[...]).astype(BF16)
    ckv = _rms(z[:, MLA_Q_RANK:MLA_Q_RANK + MLA_KV_RANK], kvn_ref[...]).astype(BF16)
    kr = _rope_apply(z[:, 512:640], c_m, s_m)

    mq0 = 640 + 3 * DIL_WIDTH
    for c0 in range(0, MEM_WIDTH, MXU_COLS):
        mq_c = _dot(h, w_ref[:, mq0 + c0:mq0 + c0 + MXU_COLS])
        mq_ref[:, c0:c0 + MXU_COLS] = (mq_c * (MEM_HEAD_DIM ** -0.5)).astype(BF16)

    d_scale = DIL_HEAD_DIM ** -0.5 * LOG2_E
    cq_d, suq_d, sdq_d = c_d * d_scale, su_d * d_scale, sd_d * d_scale
    nblk = DIL_WIDTH // LANES
    for c0 in reversed(range(0, 3 * DIL_WIDTH, MXU_COLS)):
        zd_c = _dot(h, w_ref[:, 640 + c0:640 + c0 + MXU_COLS])
        for o in range(0, MXU_COLS, LANES):
            j = (c0 + o) // LANES
            part = zd_c[:, o:o + LANES]
            if j < nblk:
                zd_ref[j] = _rope_apply_near(part, cq_d, suq_d, sdq_d)
            elif j < 2 * nblk:
                zd_ref[j] = _rope_apply_near(part, c_d, su_d, sd_d)
            else:
                zd_ref[j] = part

    outs = ((q0_ref, k0_ref, v0_ref), (q1_ref, k1_ref, v1_ref), (q2_ref, k2_ref, v2_ref))
    gblk = DIL_GROUP_WIDTH // LANES
    for g, (_, dil) in enumerate(DIL_GROUPS):
        for part in range(3):
            o_ref = outs[g][part]
            for jj in range(gblk):
                j = part * nblk + g * gblk + jj
                cols = slice(jj * LANES, (jj + 1) * LANES)
                if dil == 1:
                    o_ref[0, :, cols] = zd_ref[j].astype(BF16)
                else:
                    for r in range(dil):
                        o_ref[r, :, cols] = zd_ref[j, pl.ds(r, tm // dil, stride=dil), :].astype(BF16)


    lane = lax.broadcasted_iota(jnp.int32, (1, LANES), 1)
    ones_col = (lane == MLA_V_DIM).astype(F32)
    hw = MLA_HEADS * MLA_HEAD_PAD
    for c0 in range(0, hw, MXU_COLS):
        q_c = _dot(cq, wuq_ref[:, c0:c0 + MXU_COLS])
        k_c = _dot(ckv, wukv_ref[:, c0:c0 + MXU_COLS])
        v_c = _dot(ckv, wukv_ref[:, hw + c0:hw + c0 + MXU_COLS])
        for o in range(0, MXU_COLS, MLA_HEAD_PAD):
            src, dst = slice(o, o + MLA_HEAD_PAD), slice(c0 + o, c0 + o + MLA_HEAD_PAD)
            qm_ref[:, dst] = _rope_apply(q_c[:, src], cq_m, sq_m).astype(BF16)
            km_ref[:, dst] = (k_c[:, src] + kr).astype(BF16)
            vm_ref[:, dst] = (v_c[:, src] + ones_col).astype(BF16)


def _mixproj(x1, pos, g, w_b, qn, wuq, kvn, wukv, rope_consts, *, tm=512):
    b, s, d = x1.shape
    assert s % tm == 0 and tm % LANES == 0
    hw = MLA_HEADS * MLA_HEAD_PAD
    tok = lambda width: pl.BlockSpec((None, tm, width), lambda bi, i: (bi, i, 0))
    in_specs = [
        tok(d),
        pl.BlockSpec((None, s // LANES, LANES), lambda bi, i: (bi, 0, 0)),
        _const_spec(g.shape),
        _const_spec(w_b.shape),
        _const_spec(qn.shape),
        _const_spec(wuq.shape),
        _const_spec(kvn.shape),
        _const_spec(wukv.shape),
        _const_spec(rope_consts.shape),
    ]
    out_shapes = [jax.ShapeDtypeStruct((b, s, hw), BF16)] * 3
    out_specs = [tok(hw)] * 3
    for _, dil in DIL_GROUPS:
        for _ in range(3):
            out_shapes.append(jax.ShapeDtypeStruct((b, dil, s // dil, DIL_GROUP_WIDTH), BF16))
            out_specs.append(pl.BlockSpec((None, dil, tm // dil, DIL_GROUP_WIDTH),
                                          lambda bi, i: (bi, 0, i, 0)))
    out_shapes.append(jax.ShapeDtypeStruct((b, s, MEM_WIDTH), BF16))
    out_specs.append(tok(MEM_WIDTH))
    return pl.pallas_call(
        _mixproj_kernel,
        grid=(b, s // tm),
        in_specs=in_specs,
        out_specs=out_specs,
        out_shape=out_shapes,
        scratch_shapes=[pltpu.VMEM((3 * DIL_WIDTH // LANES, tm, LANES), F32)],
        compiler_params=_params(2),
        name="mixproj",
    )(x1, pos, g, w_b, qn, wuq, kvn, wukv, rope_consts)


def _mla_attn_kernel(q_ref, k_ref, v_ref, o_ref, *, tk, unroll):
    tq = q_ref.shape[0]
    s_len = k_ref.shape[0]
    heads = q_ref.shape[1] // MLA_HEAD_PAD
    blks = [slice(hd * MLA_HEAD_PAD, (hd + 1) * MLA_HEAD_PAD) for hd in range(heads)]

    def body(j, carry):
        rows = pl.ds(pl.multiple_of(j * tk, tk), tk)
        new = []
        for blk, (m, acc) in zip(blks, carry):
            s = _dot_nt(q_ref[:, blk], k_ref[rows, blk])
            m_new = jnp.maximum(m, jnp.max(s, axis=-1, keepdims=True))
            p = jnp.exp2(s - m_new).astype(BF16)
            acc = jnp.exp2(m - m_new) * acc + _dot(p, v_ref[rows, blk])
            new.append((m_new, acc))
        return tuple(new)

    init = tuple((jnp.full((tq, 1), NEG_INF, F32), jnp.zeros((tq, MLA_HEAD_PAD), F32))
                 for _ in range(heads))
    final = lax.fori_loop(0, s_len // tk, body, init, unroll=unroll)
    outs = [acc * (1.0 / acc[:, MLA_V_DIM:MLA_V_DIM + 1]) for _, acc in final]
    lane = lax.broadcasted_iota(jnp.int32, (1, LANES), 1)
    for pr in range(heads // 2):
        even, odd = outs[2 * pr], outs[2 * pr + 1]
        pair = jnp.where(lane < MLA_V_DIM, even, pltpu.roll(odd, MLA_V_DIM, axis=1))
        o_ref[:, pr * LANES:(pr + 1) * LANES] = pair.astype(BF16)


def _mla_attn(q, k, v, *, tq=1024, tk=2048, heads_per_step=4, unroll=2):
    b, s, hw = q.shape
    assert s % tq == 0 and s % tk == 0 and heads_per_step % 2 == 0
    wblk = heads_per_step * MLA_HEAD_PAD
    oblk = heads_per_step * MLA_V_DIM
    return pl.pallas_call(
        functools.partial(_mla_attn_kernel, tk=tk, unroll=unroll),
        grid=(b, hw // wblk, s // tq),
        in_specs=[
            pl.BlockSpec((None, tq, wblk), lambda bi, hi, i: (bi, i, hi)),
            pl.BlockSpec((None, s, wblk), lambda bi, hi, i: (bi, 0, hi)),
            pl.BlockSpec((None, s, wblk), lambda bi, hi, i: (bi, 0, hi)),
        ],
        out_specs=pl.BlockSpec((None, tq, oblk), lambda bi, hi, i: (bi, i, hi)),
        out_shape=jax.ShapeDtypeStruct((b, s, MLA_HEADS * MLA_V_DIM), BF16),
        compiler_params=_params(3),
        name="mla_attn",
    )(q, k, v)


def _dil_attn_kernel(q_ref, k_ref, v_ref, o_ref, lse_ref, *, n_side):
    n_cls, tq_all, _ = q_ref.shape
    length = k_ref.shape[1]
    i = pl.program_id(2)
    lane = lax.broadcasted_iota(jnp.int32, (1, LANES), 1)
    low = lane < DIL_HEAD_DIM
    diff = (lax.broadcasted_iota(jnp.int32, (DIL_Q_TILE, DIL_K_TILE), 1)
            - lax.broadcasted_iota(jnp.int32, (DIL_Q_TILE, DIL_K_TILE), 0))
    ones_blk = jnp.ones((DIL_K_TILE, LANES), BF16)
    for t in range(tq_all // DIL_Q_TILE):
        qs = i * tq_all + t * DIL_Q_TILE
        ks = pl.multiple_of(jnp.clip(qs - n_side, 0, length - DIL_K_TILE), n_side)
        bias = jnp.where(jnp.abs(diff + (ks - qs)) <= n_side, 0.0, NEG_INF)
        bias2 = jnp.concatenate([bias, bias], axis=0)
        rows = slice(t * DIL_Q_TILE, (t + 1) * DIL_Q_TILE)
        for c in range(n_cls):
            for pr in range(DIL_GROUP_WIDTH // LANES):
                cols = slice(pr * LANES, (pr + 1) * LANES)
                qp = q_ref[c, rows, cols]
                kp = k_ref[c, pl.ds(ks, DIL_K_TILE), cols]
                vp = v_ref[c, pl.ds(ks, DIL_K_TILE), cols]
                zero = jnp.zeros_like(qp)
                q2 = jnp.concatenate([jnp.where(low, qp, zero), jnp.where(low, zero, qp)], axis=0)
                s = _dot_nt(q2, kp) + bias2
                m = jnp.max(s, axis=-1, keepdims=True)
                p = jnp.exp2(s - m).astype(BF16)
                pv = _dot(p, jnp.concatenate([vp, ones_blk], axis=1))
                pick = lambda a: jnp.where(low, a[:DIL_Q_TILE], a[DIL_Q_TILE:])
                num, den = pick(pv[:, :LANES]), pick(pv[:, LANES:])
                o_ref[c, rows, cols] = (num * (1.0 / den)).astype(BF16)
                lse_ref[c, rows, cols] = pick(jnp.broadcast_to(m, (2 * DIL_Q_TILE, LANES))) + jnp.log2(den)


def _dil_attn(q, k, v, n_side, *, subtiles_per_step=8):
    b, dil, length, w = q.shape
    assert w == DIL_GROUP_WIDTH and n_side == DIL_Q_TILE // 2
    assert DIL_K_TILE == DIL_Q_TILE + 2 * n_side and length >= DIL_K_TILE
    tq = min(length, subtiles_per_step * DIL_Q_TILE)
    n_cls = min(dil, (subtiles_per_step * DIL_Q_TILE) // tq)
    assert length % tq == 0 and tq % DIL_Q_TILE == 0 and dil % n_cls == 0
    q_spec = pl.BlockSpec((None, n_cls, tq, w), lambda bi, r, i: (bi, r, i, 0))
    kv_spec = pl.BlockSpec((None, n_cls, length, w), lambda bi, r, i: (bi, r, 0, 0))
    return pl.pallas_call(
        functools.partial(_dil_attn_kernel, n_side=n_side),
        grid=(b, dil // n_cls, length // tq),
        in_specs=[q_spec, kv_spec, kv_spec],
        out_specs=[q_spec, q_spec],
        out_shape=[jax.ShapeDtypeStruct(q.shape, BF16), jax.ShapeDtypeStruct(q.shape, F32)],
        compiler_params=_params(3),
        name=f"dil_attn_d{dil}",
    )(q, k, v)


def _mem_kv_kernel(mem_ref, g_ref, w_ref, k_ref, v_ref):
    hm = _rms(mem_ref[...], g_ref[...]).astype(BF16)
    kv = _dot(hm, w_ref[...])
    k_ref[...] = kv[:, :MEM_WIDTH].astype(BF16)
    v_ref[...] = kv[:, MEM_WIDTH:].astype(BF16)


def _mem_kv(mem, g, w_kv):
    b, m, d = mem.shape
    spec_out = pl.BlockSpec((None, m, MEM_WIDTH), lambda bi: (bi, 0, 0))
    return pl.pallas_call(
        _mem_kv_kernel,
        grid=(b,),
        in_specs=[pl.BlockSpec((None, m, d), lambda bi: (bi, 0, 0)),
                  _const_spec(g.shape), _const_spec(w_kv.shape)],
        out_specs=[spec_out, spec_out],
        out_shape=[jax.ShapeDtypeStruct((b, m, MEM_WIDTH), BF16)] * 2,
        compiler_params=_params(1),
        name="mem_kv",
    )(mem, g, w_kv)


def _mix_kernel(x_ref, g_ref, wgate_ref, omla_ref, womla_ref,
                o0_ref, l0_ref, o1_ref, l1_ref, o2_ref, l2_ref, wodil_ref,
                mq_ref, mk_ref, mv_ref, womem_ref, wout_ref,
                out_ref, so1_ref, sl1_ref, so2_ref, sl2_ref):
    x = x_ref[...]
    d = x.shape[1]
    h = _rms(x, g_ref[...]).astype(BF16)

    o_mla = omla_ref[...]
    head_blks = [slice(hd * MEM_HEAD_DIM, (hd + 1) * MEM_HEAD_DIM) for hd in range(MEM_HEADS)]
    scores = [_dot_nt(mq_ref[:, blk], mk_ref[:, blk]) for blk in head_blks]

    col_chunks = [slice(c0, c0 + MXU_COLS) for c0 in range(0, d, MXU_COLS)]

    def gate(branch, cols):
        lo = branch * d + cols.start
        return jax.nn.sigmoid(_dot(h, wgate_ref[:, lo:lo + MXU_COLS]))

    mixed = [gate(0, cols) * _dot(o_mla, womla_ref[:, cols]) for cols in col_chunks]
    gates_dil = [gate(1, cols) for cols in col_chunks]
    gates_mem = [gate(2, cols) for cols in col_chunks]

    o_mem = []
    for blk, s in zip(head_blks, scores):
        m = jnp.max(s, axis=-1, keepdims=True)
        p = jnp.exp(s - m)
        l = jnp.sum(p, axis=-1, keepdims=True)
        o_mem.append((_dot(p.astype(BF16), mv_ref[:, blk]) * (1.0 / l)).astype(BF16))
    o_mem = jnp.concatenate(o_mem, axis=1)

    gblk = DIL_GROUP_WIDTH // LANES
    for o_ref, l_ref, so_ref, sl_ref in ((o1_ref, l1_ref, so1_ref, sl1_ref),
                                         (o2_ref, l2_ref, so2_ref, sl2_ref)):
        dil, n = o_ref.shape[0], o_ref.shape[1]
        for r in range(dil):
            rows = pl.ds(r, n, stride=dil)
            for jj in range(gblk):
                cols = slice(jj * LANES, (jj + 1) * LANES)
                so_ref[jj, rows, :] = o_ref[r, :, cols].astype(F32)
                sl_ref[jj, rows, :] = l_ref[r, :, cols]
    o_parts = []
    for jj in range(gblk):
        cols = slice(jj * LANES, (jj + 1) * LANES)
        l0, l1, l2 = l0_ref[:, cols], sl1_ref[jj], sl2_ref[jj]
        lmax = jnp.maximum(jnp.maximum(l0, l1), l2)
        w0, w1, w2 = jnp.exp2(l0 - lmax), jnp.exp2(l1 - lmax), jnp.exp2(l2 - lmax)
        o_sum = w0 * o0_ref[:, cols].astype(F32) + w1 * so1_ref[jj] + w2 * so2_ref[jj]
        o_parts.append((o_sum * (1.0 / (w0 + w1 + w2))).astype(BF16))
    o_dil = jnp.concatenate(o_parts, axis=1)

    for i, cols in enumerate(col_chunks):
        acc = mixed[i] + gates_dil[i] * _dot(o_dil, wodil_ref[:, cols])
        acc = acc + gates_mem[i] * _dot(o_mem, womem_ref[:, cols])
        mixed[i] = acc.astype(BF16)
    out_ref[...] = x + _dot(jnp.concatenate(mixed, axis=1), wout_ref[...])


def _mix(x1, g, w_gate, o_mla, wo_mla, dil_outs, wo_dil, mq, mem_k, mem_v, wo_mem, w_out, *, tm=512):
    b, s, d = x1.shape
    assert s % tm == 0
    n_mem = mem_k.shape[1]
    tok = lambda width: pl.BlockSpec((None, tm, width), lambda bi, i: (bi, i, 0))
    in_specs = [tok(d), _const_spec(g.shape), _const_spec(w_gate.shape),
                tok(o_mla.shape[2]), _const_spec(wo_mla.shape)]
    args = [x1, g, w_gate, o_mla, wo_mla]
    for (o_g, lse_g), (_, dil) in zip(dil_outs, DIL_GROUPS):
        if dil == 1:
            spec = tok(DIL_GROUP_WIDTH)
            o_g, lse_g = o_g.reshape(b, s, DIL_GROUP_WIDTH), lse_g.reshape(b, s, DIL_GROUP_WIDTH)
        else:
            spec = pl.BlockSpec((None, dil, tm // dil, DIL_GROUP_WIDTH), lambda bi, i: (bi, 0, i, 0))
        in_specs += [spec, spec]
        args += [o_g, lse_g]
    in_specs += [_const_spec(wo_dil.shape), tok(MEM_WIDTH),
                 pl.BlockSpec((None, n_mem, MEM_WIDTH), lambda bi, i: (bi, 0, 0)),
                 pl.BlockSpec((None, n_mem, MEM_WIDTH), lambda bi, i: (bi, 0, 0)),
                 _const_spec(wo_mem.shape), _const_spec(w_out.shape)]
    args += [wo_dil, mq, mem_k, mem_v, wo_mem, w_out]
    return pl.pallas_call(
        _mix_kernel,
        grid=(b, s // tm),
        in_specs=in_specs,
        out_specs=tok(d),
        out_shape=jax.ShapeDtypeStruct((b, s, d), F32),
        scratch_shapes=[pltpu.VMEM((DIL_GROUP_WIDTH // LANES, tm, LANES), F32)] * 4,
        compiler_params=_params(2),
        name="mix",
    )(*args)


def _mla_lane_map():
    half = MLA_ROPE_DIM // 2
    r1 = MLA_ROPE_LANE
    lanes = np.full(LANES, MLA_QK_DIM)
    lanes[r1:r1 + half] = MLA_NOPE_DIM + np.arange(half)
    lanes[r1 + ROPE_PARTNER_SHIFT:r1 + ROPE_PARTNER_SHIFT + half] = MLA_NOPE_DIM + half + np.arange(half)
    free = [i for i in range(MLA_QK_DIM) if lanes[i] == MLA_QK_DIM]
    lanes[free] = np.arange(MLA_NOPE_DIM)
    return lanes


def _rope_consts():
    half_m, half_d = MLA_ROPE_DIM // 2, DIL_ROPE_DIM // 2
    f_m = ROPE_THETA ** (-2.0 * jnp.arange(half_m, dtype=F32) / MLA_ROPE_DIM)
    f_d = ROPE_THETA ** (-2.0 * jnp.arange(half_d, dtype=F32) / DIL_ROPE_DIM)
    assert MLA_ROPE_LANE == 2 * half_d and MLA_ROPE_LANE + half_m <= DIL_HEAD_DIM
    half_row = jnp.concatenate([f_d, f_d, f_m, jnp.zeros((DIL_HEAD_DIM - MLA_ROPE_LANE - half_m,), F32)])
    signs = np.zeros((7, LANES), np.float32)
    for base, sign in ((0, -1.0), (ROPE_PARTNER_SHIFT, 1.0)):
        signs[0, base + MLA_ROPE_LANE:base + MLA_ROPE_LANE + half_m] = sign
        signs[1, base + half_d:base + 2 * half_d] = 1.0
        signs[2, base:base + half_d] = -1.0
    return jnp.concatenate([jnp.concatenate([half_row, half_row])[None, :], jnp.asarray(signs)], axis=0)


def _take_cols(w, idx):
    idx = np.asarray(idx)
    zero = idx < 0
    same_run = np.where(zero[1:] | zero[:-1], zero[1:] & zero[:-1], np.diff(idx) == 1)
    cuts = np.flatnonzero(~same_run) + 1
    parts = []
    for run in np.split(idx, cuts):
        if run[0] < 0:
            parts.append(jnp.zeros((w.shape[0], len(run)), w.dtype))
        else:
            parts.append(w[:, int(run[0]):int(run[-1]) + 1])
    return jnp.concatenate(parts, axis=1).astype(BF16)


def _prep_mix_weights(w_in, w_uq, w_ukv):
    n_lat = MLA_Q_RANK + MLA_KV_RANK
    kr0 = n_lat
    dil0 = kr0 + MLA_ROPE_DIM
    mq0 = dil0 + 3 * DIL_WIDTH
    gate0 = mq0 + MEM_WIDTH
    mla_map = _mla_lane_map()
    head = np.arange(MLA_HEADS)[:, None]
    kr_map = np.where((mla_map >= MLA_NOPE_DIM) & (mla_map < MLA_QK_DIM), kr0 + mla_map - MLA_NOPE_DIM, -1)
    w_b = _take_cols(w_in, np.concatenate([np.arange(n_lat), kr_map, np.arange(dil0, gate0)]))
    w_gate = w_in[:, gate0:].astype(BF16)
    uq_map = np.where(mla_map < MLA_QK_DIM, head * MLA_QK_DIM + mla_map, -1).reshape(-1)
    wuq = _take_cols(w_uq, uq_map)
    kv_w = MLA_NOPE_DIM + MLA_V_DIM
    k_map = np.where(mla_map < MLA_NOPE_DIM, head * kv_w + mla_map, -1).reshape(-1)
    v_lane = np.arange(MLA_HEAD_PAD)
    v_map = np.where(v_lane < MLA_V_DIM, head * kv_w + MLA_NOPE_DIM + v_lane, -1).reshape(-1)
    wukv = _take_cols(w_ukv, np.concatenate([k_map, v_map]))
    return w_b, w_gate, wuq, wukv


def kernel(x, mem, positions, ffn1_norm, ffn1_w_gate, ffn1_w_up, ffn1_w_down, mix_norm, w_in, mla_q_norm, mla_w_uq, mla_kv_norm, mla_w_ukv, mla_w_o, dil_w_o, mem_norm, mem_w_kv, mem_w_o, w_out, ffn2_norm, ffn2_w_gate, ffn2_w_up, ffn2_w_down, final_norm):
    b, s, d = x.shape
    depth = ffn1_norm.shape[0]
    pos = positions.astype(F32).reshape(b, s // LANES, LANES)
    rope_consts = _rope_consts()
    row = lambda v: v.reshape(1, -1)
    bf = lambda w: w.astype(BF16)
    for l in range(depth):
        x = _ffn(x.reshape(b * s, d), row(ffn1_norm[l]), bf(ffn1_w_gate[l]), bf(ffn1_w_up[l]),
                 bf(ffn1_w_down[l])).reshape(b, s, d)

        w_b, w_gate, wuq, wukv = _prep_mix_weights(w_in[l], mla_w_uq[l], mla_w_ukv[l])
        outs = _mixproj(x, pos, row(mix_norm[l]), w_b, row(mla_q_norm[l]), wuq,
                        row(mla_kv_norm[l]), wukv, rope_consts)
        q_m, k_m, v_m = outs[0:3]
        mq = outs[12]
        o_mla = _mla_attn(q_m, k_m, v_m)
        dil_outs = []
        for g, (window, dil) in enumerate(DIL_GROUPS):
            qg, kg, vg = outs[3 + 3 * g:6 + 3 * g]
            dil_outs.append(_dil_attn(qg, kg, vg, window // (2 * dil)))
        mem_k, mem_v = _mem_kv(mem, row(mem_norm[l]), bf(mem_w_kv[l]))
        x = _mix(x, row(mix_norm[l]), w_gate, o_mla, bf(mla_w_o[l]), dil_outs, bf(dil_w_o[l]),
                 mq, mem_k, mem_v, bf(mem_w_o[l]), bf(w_out[l]))

        last = l == depth - 1
        x = _ffn(x.reshape(b * s, d), row(ffn2_norm[l]), bf(ffn2_w_gate[l]), bf(ffn2_w_up[l]),
                 bf(ffn2_w_down[l]), row(final_norm) if last else None).reshape(b, s, d)
    return x
```

```python
import functools

import numpy as np
import jax
import jax.numpy as jnp
from jax import lax
from jax.experimental import pallas as pl
from jax.experimental.pallas import tpu as pltpu

F32 = jnp.float32
BF16 = jnp.bfloat16

NORM_EPS = 1e-6
NEG_INF = -1e30
ROPE_THETA = 500000.0
LOG2_E = 1.4426950408889634

LANES = 128
MXU_COLS = 256
VMEM_LIMIT_BYTES = 56 * 1024 * 1024

MLA_HEADS = 8
MLA_Q_RANK = 384
MLA_KV_RANK = 128
MLA_NOPE_DIM = 64
MLA_ROPE_DIM = 32
MLA_V_DIM = 64
MLA_QK_DIM = MLA_NOPE_DIM + MLA_ROPE_DIM
MLA_HEAD_PAD = 128
MLA_ROPE_LANE = 16
ROPE_PARTNER_SHIFT = LANES // 2

DIL_GROUPS = ((128, 1), (512, 4), (2048, 16))
DIL_HEADS = 4
DIL_HEAD_DIM = 64
DIL_ROPE_DIM = 16
DIL_GROUP_WIDTH = DIL_HEADS * DIL_HEAD_DIM
DIL_WIDTH = len(DIL_GROUPS) * DIL_GROUP_WIDTH
DIL_Q_TILE = 128
DIL_K_TILE = 256

MEM_HEADS = 4
MEM_HEAD_DIM = 128
MEM_WIDTH = MEM_HEADS * MEM_HEAD_DIM

N_BRANCHES = 3


def _dot(a, b):
    return jnp.dot(a, b, preferred_element_type=F32)


def _dot_nt(a, b):
    return lax.dot_general(a, b, (((1,), (1,)), ((), ())), preferred_element_type=F32)


def _rms(x, g):
    y = x * lax.rsqrt(jnp.mean(x * x, axis=-1, keepdims=True) + NORM_EPS)
    return y * g


def _const_spec(shape):
    nd = len(shape)
    return pl.BlockSpec(shape, lambda *_: (0,) * nd, pipeline_mode=pl.Buffered(1))


def _params(n_grid):
    return pltpu.CompilerParams(
        dimension_semantics=("arbitrary",) * n_grid,
        vmem_limit_bytes=VMEM_LIMIT_BYTES,
    )


def _ffn_kernel(x_ref, g_ref, wg_ref, wu_ref, wd_ref, *rest, f_chunk, final):
    if final:
        fg_ref, o_ref = rest
    else:
        (o_ref,) = rest
    x = x_ref[...]
    h = _rms(x, g_ref[...]).astype(BF16)
    d_ff = wg_ref.shape[1]
    acc = jnp.zeros(x.shape, F32)
    for c in range(d_ff // f_chunk):
        cols = slice(c * f_chunk, (c + 1) * f_chunk)
        a = _dot(h, wg_ref[:, cols])
        b = _dot(h, wu_ref[:, cols])
        act = (jax.nn.silu(a) * b).astype(BF16)
        acc = acc + _dot(act, wd_ref[cols, :])
    y = x + 0.5 * acc
    if final:
        y = _rms(y, fg_ref[...])
    o_ref[...] = y


def _ffn(x2d, g, wg, wu, wd, final_g=None, *, tm=1024, f_chunk=256):
    t, d = x2d.shape
    d_ff = wg.shape[1]
    assert t % tm == 0 and d_ff % f_chunk == 0
    final = final_g is not None
    in_specs = [
        pl.BlockSpec((tm, d), lambda i: (i, 0)),
        _const_spec((1, d)),
        _const_spec((d, d_ff)),
        _const_spec((d, d_ff)),
        _const_spec((d_ff, d)),
    ]
    args = [x2d, g, wg, wu, wd]
    if final:
        in_specs.append(_const_spec((1, d)))
        args.append(final_g)
    return pl.pallas_call(
        functools.partial(_ffn_kernel, f_chunk=f_chunk, final=final),
        grid=(t // tm,),
        in_specs=in_specs,
        out_specs=pl.BlockSpec((tm, d), lambda i: (i, 0)),
        out_shape=jax.ShapeDtypeStruct((t, d), F32),
        compiler_params=_params(1),
        name="ffn_final" if final else "ffn",
    )(*args)


def _rope_apply(x, c, s):
    return x * c + pltpu.roll(x, ROPE_PARTNER_SHIFT, axis=1) * s


def _rope_apply_near(x, c, s_up, s_down):
    half = DIL_ROPE_DIM // 2
    return x * c + pltpu.roll(x, half, axis=1) * s_up + pltpu.roll(x, LANES - half, axis=1) * s_down


def _mixproj_kernel(x_ref, pos_ref, g_ref, w_ref, qn_ref, wuq_ref, kvn_ref, wukv_ref, rc_ref,
                    qm_ref, km_ref, vm_ref,
                    q0_ref, k0_ref, v0_ref, q1_ref, k1_ref, v1_ref, q2_ref, k2_ref, v2_ref,
                    mq_ref, zd_ref):
    tm = x_ref.shape[0]
    h = _rms(x_ref[...], g_ref[...]).astype(BF16)
    rows_per_tile = tm // LANES
    row0 = pl.program_id(1) * rows_per_tile
    pos = jnp.concatenate(
        [jnp.broadcast_to(pos_ref[pl.ds(row0 + r, 1), :], (LANES, LANES)).T for r in range(rows_per_tile)],
        axis=0)
    rc = rc_ref[...]

    ang = pos * rc[0:1]
    cos_t, sin_t = jnp.cos(ang), jnp.sin(ang)
    sgn_m, up_d, down_d = rc[1:2], rc[2:3], rc[3:4]
    c_m, s_m = jnp.where((up_d != 0.0) | (down_d != 0.0), 1.0, cos_t), sin_t * sgn_m
    c_d, su_d, sd_d = jnp.where(sgn_m != 0.0, 1.0, cos_t), sin_t * up_d, sin_t * down_d

    q_scale = MLA_QK_DIM ** -0.5 * LOG2_E
    cq_m, sq_m = c_m * q_scale, s_m * q_scale

    z = _dot(h, w_ref[:, 0:640])
    cq = _rms(z[:, 0:MLA_Q_RANK], qn_ref[...]).astype(BF16)
    ckv = _rms(z[:, MLA_Q_RANK:MLA_Q_RANK + MLA_KV_RANK], kvn_ref[...]).astype(BF16)
    kr = _rope_apply(z[:, 512:640], c_m, s_m)

    mq0 = 640 + 3 * DIL_WIDTH
    for c0 in range(0, MEM_WIDTH, MXU_COLS):
        mq_c = _dot(h, w_ref[:, mq0 + c0:mq0 + c0 + MXU_COLS])
        mq_ref[:, c0:c0 + MXU_COLS] = (mq_c * (MEM_HEAD_DIM ** -0.5)).astype(BF16)

    d_scale = DIL_HEAD_DIM ** -0.5 * LOG2_E
    cq_d, suq_d, sdq_d = c_d * d_scale, su_d * d_scale, sd_d * d_scale
    nblk = DIL_WIDTH // LANES
    for c0 in reversed(range(0, 3 * DIL_WIDTH, MXU_COLS)):
        zd_c = _dot(h, w_ref[:, 640 + c0:640 + c0 + MXU_COLS])
        for o in range(0, MXU_COLS, LANES):
            j = (c0 + o) // LANES
            part = zd_c[:, o:o + LANES]
            if j < nblk:
                zd_ref[j] = _rope_apply_near(part, cq_d, suq_d, sdq_d)
            elif j < 2 * nblk:
                zd_ref[j] = _rope_apply_near(part, c_d, su_d, sd_d)
            else:
                zd_ref[j] = part

    outs = ((q0_ref, k0_ref, v0_ref), (q1_ref, k1_ref, v1_ref), (q2_ref, k2_ref, v2_ref))
    gblk = DIL_GROUP_WIDTH // LANES
    for g, (_, dil) in enumerate(DIL_GROUPS):
        for part in range(3):
            o_ref = outs[g][part]
            for jj in range(gblk):
                j = part * nblk + g * gblk + jj
                cols = slice(jj * LANES, (jj + 1) * LANES)
                if dil == 1:
                    o_ref[0, :, cols] = zd_ref[j].astype(BF16)
                else:
                    for r in range(dil):
                        o_ref[r, :, cols] = zd_ref[j, pl.ds(r, tm // dil, stride=dil), :].astype(BF16)


    lane = lax.broadcasted_iota(jnp.int32, (1, LANES), 1)
    ones_col = (lane == MLA_V_DIM).astype(F32)
    hw = MLA_HEADS * MLA_HEAD_PAD
    for c0 in range(0, hw, MXU_COLS):
        q_c = _dot(cq, wuq_ref[:, c0:c0 + MXU_COLS])
        k_c = _dot(ckv, wukv_ref[:, c0:c0 + MXU_COLS])
        v_c = _dot(ckv, wukv_ref[:, hw + c0:hw + c0 + MXU_COLS])
        for o in range(0, MXU_COLS, MLA_HEAD_PAD):
            src, dst = slice(o, o + MLA_HEAD_PAD), slice(c0 + o, c0 + o + MLA_HEAD_PAD)
            qm_ref[:, dst] = _rope_apply(q_c[:, src], cq_m, sq_m).astype(BF16)
            km_ref[:, dst] = (k_c[:, src] + kr).astype(BF16)
            vm_ref[:, dst] = (v_c[:, src] + ones_col).astype(BF16)


def _mixproj(x1, pos, g, w_b, qn, wuq, kvn, wukv, rope_consts, *, tm=512):
    b, s, d = x1.shape
    assert s % tm == 0 and tm % LANES == 0
    hw = MLA_HEADS * MLA_HEAD_PAD
    tok = lambda width: pl.BlockSpec((None, tm, width), lambda bi, i: (bi, i, 0))
    in_specs = [
        tok(d),
        pl.BlockSpec((None, s // LANES, LANES), lambda bi, i: (bi, 0, 0)),
        _const_spec(g.shape),
        _const_spec(w_b.shape),
        _const_spec(qn.shape),
        _const_spec(wuq.shape),
        _const_spec(kvn.shape),
        _const_spec(wukv.shape),
        _const_spec(rope_consts.shape),
    ]
    out_shapes = [jax.ShapeDtypeStruct((b, s, hw), BF16)] * 3
    out_specs = [tok(hw)] * 3
    for _, dil in DIL_GROUPS:
        for _ in range(3):
            out_shapes.append(jax.ShapeDtypeStruct((b, dil, s // dil, DIL_GROUP_WIDTH), BF16))
            out_specs.append(pl.BlockSpec((None, dil, tm // dil, DIL_GROUP_WIDTH),
                                          lambda bi, i: (bi, 0, i, 0)))
    out_shapes.append(jax.ShapeDtypeStruct((b, s, MEM_WIDTH), BF16))
    out_specs.append(tok(MEM_WIDTH))
    return pl.pallas_call(
        _mixproj_kernel,
        grid=(b, s // tm),
        in_specs=in_specs,
        out_specs=out_specs,
        out_shape=out_shapes,
        scratch_shapes=[pltpu.VMEM((3 * DIL_WIDTH // LANES, tm, LANES), F32)],
        compiler_params=_params(2),
        name="mixproj",
    )(x1, pos, g, w_b, qn, wuq, kvn, wukv, rope_consts)


def _mla_attn_kernel(q_ref, k_ref, v_ref, o_ref, *, tk, unroll):
    tq = q_ref.shape[0]
    s_len = k_ref.shape[0]
    heads = q_ref.shape[1] // MLA_HEAD_PAD
    blks = [slice(hd * MLA_HEAD_PAD, (hd + 1) * MLA_HEAD_PAD) for hd in range(heads)]

    def body(j, carry):
        rows = pl.ds(pl.multiple_of(j * tk, tk), tk)
        new = []
        for blk, (m, acc) in zip(blks, carry):
            s = _dot_nt(q_ref[:, blk], k_ref[rows, blk])
            m_new = jnp.maximum(m, jnp.max(s, axis=-1, keepdims=True))
            p = jnp.exp2(s - m_new).astype(BF16)
            acc = jnp.exp2(m - m_new) * acc + _dot(p, v_ref[rows, blk])
            new.append((m_new, acc))
        return tuple(new)

    init = tuple((jnp.full((tq, 1), NEG_INF, F32), jnp.zeros((tq, MLA_HEAD_PAD), F32))
                 for _ in range(heads))
    final = lax.fori_loop(0, s_len // tk, body, init, unroll=unroll)
    outs = [acc * (1.0 / acc[:, MLA_V_DIM:MLA_V_DIM + 1]) for _, acc in final]
    lane = lax.broadcasted_iota(jnp.int32, (1, LANES), 1)
    for pr in range(heads // 2):
        even, odd = outs[2 * pr], outs[2 * pr + 1]
        pair = jnp.where(lane < MLA_V_DIM, even, pltpu.roll(odd, MLA_V_DIM, axis=1))
        o_ref[:, pr * LANES:(pr + 1) * LANES] = pair.astype(BF16)


def _mla_attn(q, k, v, *, tq=1024, tk=2048, heads_per_step=4, unroll=2):
    b, s, hw = q.shape
    assert s % tq == 0 and s % tk == 0 and heads_per_step % 2 == 0
    wblk = heads_per_step * MLA_HEAD_PAD
    oblk = heads_per_step * MLA_V_DIM
    return pl.pallas_call(
        functools.partial(_mla_attn_kernel, tk=tk, unroll=unroll),
        grid=(b, hw // wblk, s // tq),
        in_specs=[
            pl.BlockSpec((None, tq, wblk), lambda bi, hi, i: (bi, i, hi)),
            pl.BlockSpec((None, s, wblk), lambda bi, hi, i: (bi, 0, hi)),
            pl.BlockSpec((None, s, wblk), lambda bi, hi, i: (bi, 0, hi)),
        ],
        out_specs=pl.BlockSpec((None, tq, oblk), lambda bi, hi, i: (bi, i, hi)),
        out_shape=jax.ShapeDtypeStruct((b, s, MLA_HEADS * MLA_V_DIM), BF16),
        compiler_params=_params(3),
        name="mla_attn",
    )(q, k, v)


def _dil_attn_kernel(q_ref, k_ref, v_ref, o_ref, lse_ref, *, n_side):
    n_cls, tq_all, _ = q_ref.shape
    length = k_ref.shape[1]
    i = pl.program_id(2)
    lane = lax.broadcasted_iota(jnp.int32, (1, LANES), 1)
    low = lane < DIL_HEAD_DIM
    diff = (lax.broadcasted_iota(jnp.int32, (DIL_Q_TILE, DIL_K_TILE), 1)
            - lax.broadcasted_iota(jnp.int32, (DIL_Q_TILE, DIL_K_TILE), 0))
    ones_blk = jnp.ones((DIL_K_TILE, LANES), BF16)
    for t in range(tq_all // DIL_Q_TILE):
        qs = i * tq_all + t * DIL_Q_TILE
        ks = pl.multiple_of(jnp.clip(qs - n_side, 0, length - DIL_K_TILE), n_side)
        bias = jnp.where(jnp.abs(diff + (ks - qs)) <= n_side, 0.0, NEG_INF)
        bias2 = jnp.concatenate([bias, bias], axis=0)
        rows = slice(t * DIL_Q_TILE, (t + 1) * DIL_Q_TILE)
        for c in range(n_cls):
            for pr in range(DIL_GROUP_WIDTH // LANES):
                cols = slice(pr * LANES, (pr + 1) * LANES)
                qp = q_ref[c, rows, cols]
                kp = k_ref[c, pl.ds(ks, DIL_K_TILE), cols]
                vp = v_ref[c, pl.ds(ks, DIL_K_TILE), cols]
                zero = jnp.zeros_like(qp)
                q2 = jnp.concatenate([jnp.where(low, qp, zero), jnp.where(low, zero, qp)], axis=0)
                s = _dot_nt(q2, kp) + bias2
                m = jnp.max(s, axis=-1, keepdims=True)
                p = jnp.exp2(s - m).astype(BF16)
                pv = _dot(p, jnp.concatenate([vp, ones_blk], axis=1))
                pick = lambda a: jnp.where(low, a[:DIL_Q_TILE], a[DIL_Q_TILE:])
                num, den = pick(pv[:, :LANES]), pick(pv[:, LANES:])
                o_ref[c, rows, cols] = (num * (1.0 / den)).astype(BF16)
                lse_ref[c, rows, cols] = pick(jnp.broadcast_to(m, (2 * DIL_Q_TILE, LANES))) + jnp.log2(den)


def _dil_attn(q, k, v, n_side, *, subtiles_per_step=32):
    b, dil, length, w = q.shape
    assert w == DIL_GROUP_WIDTH and n_side == DIL_Q_TILE // 2
    assert DIL_K_TILE == DIL_Q_TILE + 2 * n_side and length >= DIL_K_TILE
    tq = min(length, subtiles_per_step * DIL_Q_TILE)
    n_cls = min(dil, (subtiles_per_step * DIL_Q_TILE) // tq)
    assert length % tq == 0 and tq % DIL_Q_TILE == 0 and dil % n_cls == 0
    q_spec = pl.BlockSpec((None, n_cls, tq, w), lambda bi, r, i: (bi, r, i, 0))
    kv_spec = pl.BlockSpec((None, n_cls, length, w), lambda bi, r, i: (bi, r, 0, 0))
    return pl.pallas_call(
        functools.partial(_dil_attn_kernel, n_side=n_side),
        grid=(b, dil // n_cls, length // tq),
        in_specs=[q_spec, kv_spec, kv_spec],
        out_specs=[q_spec, q_spec],
        out_shape=[jax.ShapeDtypeStruct(q.shape, BF16), jax.ShapeDtypeStruct(q.shape, F32)],
        compiler_params=_params(3),
        name=f"dil_attn_d{dil}",
    )(q, k, v)


def _mem_kv_kernel(mem_ref, g_ref, w_ref, k_ref, v_ref):
    hm = _rms(mem_ref[...], g_ref[...]).astype(BF16)
    kv = _dot(hm, w_ref[...])
    k_ref[...] = kv[:, :MEM_WIDTH].astype(BF16)
    v_ref[...] = kv[:, MEM_WIDTH:].astype(BF16)


def _mem_kv(mem, g, w_kv):
    b, m, d = mem.shape
    spec_out = pl.BlockSpec((None, m, MEM_WIDTH), lambda bi: (bi, 0, 0))
    return pl.pallas_call(
        _mem_kv_kernel,
        grid=(b,),
        in_specs=[pl.BlockSpec((None, m, d), lambda bi: (bi, 0, 0)),
                  _const_spec(g.shape), _const_spec(w_kv.shape)],
        out_specs=[spec_out, spec_out],
        out_shape=[jax.ShapeDtypeStruct((b, m, MEM_WIDTH), BF16)] * 2,
        compiler_params=_params(1),
        name="mem_kv",
    )(mem, g, w_kv)


def _mix_kernel(x_ref, g_ref, wgate_ref, omla_ref, womla_ref,
                o0_ref, l0_ref, o1_ref, l1_ref, o2_ref, l2_ref, wodil_ref,
                mq_ref, mk_ref, mv_ref, womem_ref, wout_ref,
                out_ref, so1_ref, sl1_ref, so2_ref, sl2_ref):
    x = x_ref[...]
    d = x.shape[1]
    h = _rms(x, g_ref[...]).astype(BF16)

    o_mla = omla_ref[...]
    head_blks = [slice(hd * MEM_HEAD_DIM, (hd + 1) * MEM_HEAD_DIM) for hd in range(MEM_HEADS)]
    scores = [_dot_nt(mq_ref[:, blk], mk_ref[:, blk]) for blk in head_blks]

    col_chunks = [slice(c0, c0 + MXU_COLS) for c0 in range(0, d, MXU_COLS)]

    def gate(branch, cols):
        lo = branch * d + cols.start
        return jax.nn.sigmoid(_dot(h, wgate_ref[:, lo:lo + MXU_COLS]))

    mixed = [gate(0, cols) * _dot(o_mla, womla_ref[:, cols]) for cols in col_chunks]
    gates_dil = [gate(1, cols) for cols in col_chunks]
    gates_mem = [gate(2, cols) for cols in col_chunks]

    o_mem = []
    for blk, s in zip(head_blks, scores):
        m = jnp.max(s, axis=-1, keepdims=True)
        p = jnp.exp(s - m)
        l = jnp.sum(p, axis=-1, keepdims=True)
        o_mem.append((_dot(p.astype(BF16), mv_ref[:, blk]) * (1.0 / l)).astype(BF16))
    o_mem = jnp.concatenate(o_mem, axis=1)

    gblk = DIL_GROUP_WIDTH // LANES
    for o_ref, l_ref, so_ref, sl_ref in ((o1_ref, l1_ref, so1_ref, sl1_ref),
                                         (o2_ref, l2_ref, so2_ref, sl2_ref)):
        dil, n = o_ref.shape[0], o_ref.shape[1]
        for r in range(dil):
            rows = pl.ds(r, n, stride=dil)
            for jj in range(gblk):
                cols = slice(jj * LANES, (jj + 1) * LANES)
                so_ref[jj, rows, :] = o_ref[r, :, cols].astype(F32)
                sl_ref[jj, rows, :] = l_ref[r, :, cols]
    o_parts = []
    for jj in range(gblk):
        cols = slice(jj * LANES, (jj + 1) * LANES)
        l0, l1, l2 = l0_ref[:, cols], sl1_ref[jj], sl2_ref[jj]
        lmax = jnp.maximum(jnp.maximum(l0, l1), l2)
        w0, w1, w2 = jnp.exp2(l0 - lmax), jnp.exp2(l1 - lmax), jnp.exp2(l2 - lmax)
        o_sum = w0 * o0_ref[:, cols].astype(F32) + w1 * so1_ref[jj] + w2 * so2_ref[jj]
        o_parts.append((o_sum * (1.0 / (w0 + w1 + w2))).astype(BF16))
    o_dil = jnp.concatenate(o_parts, axis=1)

    for i, cols in enumerate(col_chunks):
        acc = mixed[i] + gates_dil[i] * _dot(o_dil, wodil_ref[:, cols])
        acc = acc + gates_mem[i] * _dot(o_mem, womem_ref[:, cols])
        mixed[i] = acc.astype(BF16)
    out_ref[...] = x + _dot(jnp.concatenate(mixed, axis=1), wout_ref[...])


def _mix(x1, g, w_gate, o_mla, wo_mla, dil_outs, wo_dil, mq, mem_k, mem_v, wo_mem, w_out, *, tm=512):
    b, s, d = x1.shape
    assert s % tm == 0
    n_mem = mem_k.shape[1]
    tok = lambda width: pl.BlockSpec((None, tm, width), lambda bi, i: (bi, i, 0))
    in_specs = [tok(d), _const_spec(g.shape), _const_spec(w_gate.shape),
                tok(o_mla.shape[2]), _const_spec(wo_mla.shape)]
    args = [x1, g, w_gate, o_mla, wo_mla]
    for (o_g, lse_g), (_, dil) in zip(dil_outs, DIL_GROUPS):
        if dil == 1:
            spec = tok(DIL_GROUP_WIDTH)
            o_g, lse_g = o_g.reshape(b, s, DIL_GROUP_WIDTH), lse_g.reshape(b, s, DIL_GROUP_WIDTH)
        else:
            spec = pl.BlockSpec((None, dil, tm // dil, DIL_GROUP_WIDTH), lambda bi, i: (bi, 0, i, 0))
        in_specs += [spec, spec]
        args += [o_g, lse_g]
    in_specs += [_const_spec(wo_dil.shape), tok(MEM_WIDTH),
                 pl.BlockSpec((None, n_mem, MEM_WIDTH), lambda bi, i: (bi, 0, 0)),
                 pl.BlockSpec((None, n_mem, MEM_WIDTH), lambda bi, i: (bi, 0, 0)),
                 _const_spec(wo_mem.shape), _const_spec(w_out.shape)]
    args += [wo_dil, mq, mem_k, mem_v, wo_mem, w_out]
    return pl.pallas_call(
        _mix_kernel,
        grid=(b, s // tm),
        in_specs=in_specs,
        out_specs=tok(d),
        out_shape=jax.ShapeDtypeStruct((b, s, d), F32),
        scratch_shapes=[pltpu.VMEM((DIL_GROUP_WIDTH // LANES, tm, LANES), F32)] * 4,
        compiler_params=_params(2),
        name="mix",
    )(*args)


def _mla_lane_map():
    half = MLA_ROPE_DIM // 2
    r1 = MLA_ROPE_LANE
    lanes = np.full(LANES, MLA_QK_DIM)
    lanes[r1:r1 + half] = MLA_NOPE_DIM + np.arange(half)
    lanes[r1 + ROPE_PARTNER_SHIFT:r1 + ROPE_PARTNER_SHIFT + half] = MLA_NOPE_DIM + half + np.arange(half)
    free = [i for i in range(MLA_QK_DIM) if lanes[i] == MLA_QK_DIM]
    lanes[free] = np.arange(MLA_NOPE_DIM)
    return lanes


def _rope_consts():
    half_m, half_d = MLA_ROPE_DIM // 2, DIL_ROPE_DIM // 2
    f_m = ROPE_THETA ** (-2.0 * jnp.arange(half_m, dtype=F32) / MLA_ROPE_DIM)
    f_d = ROPE_THETA ** (-2.0 * jnp.arange(half_d, dtype=F32) / DIL_ROPE_DIM)
    assert MLA_ROPE_LANE == 2 * half_d and MLA_ROPE_LANE + half_m <= DIL_HEAD_DIM
    half_row = jnp.concatenate([f_d, f_d, f_m, jnp.zeros((DIL_HEAD_DIM - MLA_ROPE_LANE - half_m,), F32)])
    signs = np.zeros((7, LANES), np.float32)
    for base, sign in ((0, -1.0), (ROPE_PARTNER_SHIFT, 1.0)):
        signs[0, base + MLA_ROPE_LANE:base + MLA_ROPE_LANE + half_m] = sign
        signs[1, base + half_d:base + 2 * half_d] = 1.0
        signs[2, base:base + half_d] = -1.0
    return jnp.concatenate([jnp.concatenate([half_row, half_row])[None, :], jnp.asarray(signs)], axis=0)


def _take_cols(w, idx):
    idx = np.asarray(idx)
    zero = idx < 0
    same_run = np.where(zero[1:] | zero[:-1], zero[1:] & zero[:-1], np.diff(idx) == 1)
    cuts = np.flatnonzero(~same_run) + 1
    parts = []
    for run in np.split(idx, cuts):
        if run[0] < 0:
            parts.append(jnp.zeros(w.shape[:-1] + (len(run),), w.dtype))
        else:
            parts.append(w[..., int(run[0]):int(run[-1]) + 1])
    return jnp.concatenate(parts, axis=-1).astype(BF16)


def _prep_mix_weights(w_in, w_uq, w_ukv):
    n_lat = MLA_Q_RANK + MLA_KV_RANK
    kr0 = n_lat
    dil0 = kr0 + MLA_ROPE_DIM
    mq0 = dil0 + 3 * DIL_WIDTH
    gate0 = mq0 + MEM_WIDTH
    mla_map = _mla_lane_map()
    kr_map = np.where((mla_map >= MLA_NOPE_DIM) & (mla_map < MLA_QK_DIM), kr0 + mla_map - MLA_NOPE_DIM, -1)
    w_b = _take_cols(w_in, np.concatenate([np.arange(n_lat), kr_map, np.arange(dil0, gate0)]))
    w_gate = w_in[:, gate0:].astype(BF16)
    hw = MLA_HEADS * MLA_HEAD_PAD
    uq_map = np.where(mla_map < MLA_QK_DIM, mla_map, -1)
    wuq = _take_cols(w_uq.reshape(MLA_Q_RANK, MLA_HEADS, MLA_QK_DIM), uq_map).reshape(MLA_Q_RANK, hw)
    wkv = w_ukv.reshape(MLA_KV_RANK, MLA_HEADS, MLA_NOPE_DIM + MLA_V_DIM)
    k_map = np.where(mla_map < MLA_NOPE_DIM, mla_map, -1)
    v_lane = np.arange(MLA_HEAD_PAD)
    v_map = np.where(v_lane < MLA_V_DIM, MLA_NOPE_DIM + v_lane, -1)
    wukv = jnp.concatenate([_take_cols(wkv, k_map).reshape(MLA_KV_RANK, hw),
                            _take_cols(wkv, v_map).reshape(MLA_KV_RANK, hw)], axis=1)
    return w_b, w_gate, wuq, wukv


def kernel(x, mem, positions, ffn1_norm, ffn1_w_gate, ffn1_w_up, ffn1_w_down, mix_norm, w_in, mla_q_norm, mla_w_uq, mla_kv_norm, mla_w_ukv, mla_w_o, dil_w_o, mem_norm, mem_w_kv, mem_w_o, w_out, ffn2_norm, ffn2_w_gate, ffn2_w_up, ffn2_w_down, final_norm):
    b, s, d = x.shape
    depth = ffn1_norm.shape[0]
    pos = positions.astype(F32).reshape(b, s // LANES, LANES)
    rope_consts = _rope_consts()
    row = lambda v: v.reshape(1, -1)
    bf = lambda w: w.astype(BF16)
    for l in range(depth):
        x = _ffn(x.reshape(b * s, d), row(ffn1_norm[l]), bf(ffn1_w_gate[l]), bf(ffn1_w_up[l]),
                 bf(ffn1_w_down[l])).reshape(b, s, d)

        w_b, w_gate, wuq, wukv = _prep_mix_weights(w_in[l], mla_w_uq[l], mla_w_ukv[l])
        outs = _mixproj(x, pos, row(mix_norm[l]), w_b, row(mla_q_norm[l]), wuq,
                        row(mla_kv_norm[l]), wukv, rope_consts)
        q_m, k_m, v_m = outs[0:3]
        mq = outs[12]
        o_mla = _mla_attn(q_m, k_m, v_m)
        dil_outs = []
        for g, (window, dil) in enumerate(DIL_GROUPS):
            qg, kg, vg = outs[3 + 3 * g:6 + 3 * g]
            dil_outs.append(_dil_attn(qg, kg, vg, window // (2 * dil)))
        mem_k, mem_v = _mem_kv(mem, row(mem_norm[l]), bf(mem_w_kv[l]))
        x = _mix(x, row(mix_norm[l]), w_gate, o_mla, bf(mla_w_o[l]), dil_outs, bf(dil_w_o[l]),
                 mq, mem_k, mem_v, bf(mem_w_o[l]), bf(w_out[l]))

        last = l == depth - 1
        x = _ffn(x.reshape(b * s, d), row(ffn2_norm[l]), bf(ffn2_w_gate[l]), bf(ffn2_w_up[l]),
                 bf(ffn2_w_down[l]), row(final_norm) if last else None).reshape(b, s, d)
    return x
```

```python
import functools

import numpy as np
import jax
import jax.numpy as jnp
from jax import lax
from jax.experimental import pallas as pl
from jax.experimental.pallas import tpu as pltpu

F32 = jnp.float32
BF16 = jnp.bfloat16

NORM_EPS = 1e-6
NEG_INF = -1e30
ROPE_THETA = 500000.0
LOG2_E = 1.4426950408889634

LANES = 128
MXU_COLS = 256
VMEM_LIMIT_BYTES = 56 * 1024 * 1024

MLA_HEADS = 8
MLA_Q_RANK = 384
MLA_KV_RANK = 128
MLA_NOPE_DIM = 64
MLA_ROPE_DIM = 32
MLA_V_DIM = 64
MLA_QK_DIM = MLA_NOPE_DIM + MLA_ROPE_DIM
MLA_HEAD_PAD = 128
MLA_ROPE_LANE = 16
ROPE_PARTNER_SHIFT = LANES // 2

DIL_GROUPS = ((128, 1), (512, 4), (2048, 16))
DIL_HEADS = 4
DIL_HEAD_DIM = 64
DIL_ROPE_DIM = 16
DIL_GROUP_WIDTH = DIL_HEADS * DIL_HEAD_DIM
DIL_WIDTH = len(DIL_GROUPS) * DIL_GROUP_WIDTH
DIL_Q_TILE = 128
DIL_K_TILE = 256

MEM_HEADS = 4
MEM_HEAD_DIM = 128
MEM_WIDTH = MEM_HEADS * MEM_HEAD_DIM

N_BRANCHES = 3


def _dot(a, b):
    return jnp.dot(a, b, preferred_element_type=F32)


def _dot_nt(a, b):
    return lax.dot_general(a, b, (((1,), (1,)), ((), ())), preferred_element_type=F32)


def _rms(x, g):
    y = x * lax.rsqrt(jnp.mean(x * x, axis=-1, keepdims=True) + NORM_EPS)
    return y * g


def _const_spec(shape):
    nd = len(shape)
    return pl.BlockSpec(shape, lambda *_: (0,) * nd, pipeline_mode=pl.Buffered(1))


def _params(n_grid):
    return pltpu.CompilerParams(
        dimension_semantics=("arbitrary",) * n_grid,
        vmem_limit_bytes=VMEM_LIMIT_BYTES,
    )


def _ffn_kernel(x_ref, g_ref, wg_ref, wu_ref, wd_ref, *rest, f_chunk, final):
    if final:
        fg_ref, o_ref = rest
    else:
        (o_ref,) = rest
    x = x_ref[...]
    h = _rms(x, g_ref[...]).astype(BF16)
    d_ff = wg_ref.shape[1]
    acc = jnp.zeros(x.shape, F32)
    for c in range(d_ff // f_chunk):
        cols = slice(c * f_chunk, (c + 1) * f_chunk)
        a = _dot(h, wg_ref[:, cols])
        b = _dot(h, wu_ref[:, cols])
        act = (jax.nn.silu(a) * b).astype(BF16)
        acc = acc + _dot(act, wd_ref[cols, :])
    y = x + 0.5 * acc
    if final:
        y = _rms(y, fg_ref[...])
    o_ref[...] = y


def _ffn(x2d, g, wg, wu, wd, final_g=None, *, tm=1024, f_chunk=256):
    t, d = x2d.shape
    d_ff = wg.shape[1]
    assert t % tm == 0 and d_ff % f_chunk == 0
    final = final_g is not None
    in_specs = [
        pl.BlockSpec((tm, d), lambda i: (i, 0)),
        _const_spec((1, d)),
        _const_spec((d, d_ff)),
        _const_spec((d, d_ff)),
        _const_spec((d_ff, d)),
    ]
    args = [x2d, g, wg, wu, wd]
    if final:
        in_specs.append(_const_spec((1, d)))
        args.append(final_g)
    return pl.pallas_call(
        functools.partial(_ffn_kernel, f_chunk=f_chunk, final=final),
        grid=(t // tm,),
        in_specs=in_specs,
        out_specs=pl.BlockSpec((tm, d), lambda i: (i, 0)),
        out_shape=jax.ShapeDtypeStruct((t, d), F32),
        compiler_params=_params(1),
        name="ffn_final" if final else "ffn",
    )(*args)


def _rope_apply(x, c, s):
    return x * c + pltpu.roll(x, ROPE_PARTNER_SHIFT, axis=1) * s


def _rope_apply_near(x, c, s_up, s_down):
    half = DIL_ROPE_DIM // 2
    return x * c + pltpu.roll(x, half, axis=1) * s_up + pltpu.roll(x, LANES - half, axis=1) * s_down


def _mixproj_kernel(x_ref, pos_ref, g_ref, w_ref, qn_ref, wuq_ref, kvn_ref, wukv_ref, rc_ref,
                    qm_ref, km_ref, vm_ref,
                    q0_ref, k0_ref, v0_ref, q1_ref, k1_ref, v1_ref, q2_ref, k2_ref, v2_ref,
                    mq_ref, zd_ref):
    tm = x_ref.shape[0]
    h = _rms(x_ref[...], g_ref[...]).astype(BF16)
    rows_per_tile = tm // LANES
    row0 = pl.program_id(1) * rows_per_tile
    pos = jnp.concatenate(
        [jnp.broadcast_to(pos_ref[pl.ds(row0 + r, 1), :], (LANES, LANES)).T for r in range(rows_per_tile)],
        axis=0)
    rc = rc_ref[...]

    ang = pos * rc[0:1]
    cos_t, sin_t = jnp.cos(ang), jnp.sin(ang)
    sgn_m, up_d, down_d = rc[1:2], rc[2:3], rc[3:4]
    c_m, s_m = jnp.where((up_d != 0.0) | (down_d != 0.0), 1.0, cos_t), sin_t * sgn_m
    c_d, su_d, sd_d = jnp.where(sgn_m != 0.0, 1.0, cos_t), sin_t * up_d, sin_t * down_d

    q_scale = MLA_QK_DIM ** -0.5 * LOG2_E
    cq_m, sq_m = c_m * q_scale, s_m * q_scale

    z = _dot(h, w_ref[:, 0:640])
    cq = _rms(z[:, 0:MLA_Q_RANK], qn_ref[...]).astype(BF16)
    ckv = _rms(z[:, MLA_Q_RANK:MLA_Q_RANK + MLA_KV_RANK], kvn_ref[...]).astype(BF16)
    kr = _rope_apply(z[:, 512:640], c_m, s_m)

    mq0 = 640 + 3 * DIL_WIDTH
    for c0 in range(0, MEM_WIDTH, MXU_COLS):
        mq_c = _dot(h, w_ref[:, mq0 + c0:mq0 + c0 + MXU_COLS])
        mq_ref[:, c0:c0 + MXU_COLS] = (mq_c * (MEM_HEAD_DIM ** -0.5)).astype(BF16)

    d_scale = DIL_HEAD_DIM ** -0.5 * LOG2_E
    cq_d, suq_d, sdq_d = c_d * d_scale, su_d * d_scale, sd_d * d_scale
    outs = ((q0_ref, k0_ref, v0_ref), (q1_ref, k1_ref, v1_ref), (q2_ref, k2_ref, v2_ref))
    nblk = DIL_WIDTH // LANES
    gblk = DIL_GROUP_WIDTH // LANES
    for c0 in reversed(range(0, 3 * DIL_WIDTH, MXU_COLS)):
        zd_c = _dot(h, w_ref[:, 640 + c0:640 + c0 + MXU_COLS])
        for o in range(0, MXU_COLS, LANES):
            j = (c0 + o) // LANES
            part, g, jj = j // nblk, (j % nblk) // gblk, j % gblk
            blk = zd_c[:, o:o + LANES]
            if part == 0:
                blk = _rope_apply_near(blk, cq_d, suq_d, sdq_d)
            elif part == 1:
                blk = _rope_apply_near(blk, c_d, su_d, sd_d)
            o_ref, dil = outs[g][part], DIL_GROUPS[g][1]
            cols = slice(jj * LANES, (jj + 1) * LANES)
            if dil == 1:
                o_ref[0, :, cols] = blk.astype(BF16)
            else:
                slab = part * (nblk - gblk) + (g - 1) * gblk + jj
                zd_ref[slab] = blk
                for r in range(dil):
                    o_ref[r, :, cols] = zd_ref[slab, pl.ds(r, tm // dil, stride=dil), :].astype(BF16)


    lane = lax.broadcasted_iota(jnp.int32, (1, LANES), 1)
    ones_col = (lane == MLA_V_DIM).astype(F32)
    hw = MLA_HEADS * MLA_HEAD_PAD
    for c0 in range(0, hw, MXU_COLS):
        q_c = _dot(cq, wuq_ref[:, c0:c0 + MXU_COLS])
        k_c = _dot(ckv, wukv_ref[:, c0:c0 + MXU_COLS])
        v_c = _dot(ckv, wukv_ref[:, hw + c0:hw + c0 + MXU_COLS])
        for o in range(0, MXU_COLS, MLA_HEAD_PAD):
            src, dst = slice(o, o + MLA_HEAD_PAD), slice(c0 + o, c0 + o + MLA_HEAD_PAD)
            qm_ref[:, dst] = _rope_apply(q_c[:, src], cq_m, sq_m).astype(BF16)
            km_ref[:, dst] = (k_c[:, src] + kr).astype(BF16)
            vm_ref[:, dst] = (v_c[:, src] + ones_col).astype(BF16)


def _mixproj(x1, pos, g, w_b, qn, wuq, kvn, wukv, rope_consts, *, tm=1024):
    b, s, d = x1.shape
    assert s % tm == 0 and tm % LANES == 0 and DIL_GROUPS[0][1] == 1
    hw = MLA_HEADS * MLA_HEAD_PAD
    tok = lambda width: pl.BlockSpec((None, tm, width), lambda bi, i: (bi, i, 0))
    in_specs = [
        tok(d),
        pl.BlockSpec((None, s // LANES, LANES), lambda bi, i: (bi, 0, 0)),
        _const_spec(g.shape),
        _const_spec(w_b.shape),
        _const_spec(qn.shape),
        _const_spec(wuq.shape),
        _const_spec(kvn.shape),
        _const_spec(wukv.shape),
        _const_spec(rope_consts.shape),
    ]
    out_shapes = [jax.ShapeDtypeStruct((b, s, hw), BF16)] * 3
    out_specs = [tok(hw)] * 3
    for _, dil in DIL_GROUPS:
        for _ in range(3):
            out_shapes.append(jax.ShapeDtypeStruct((b, dil, s // dil, DIL_GROUP_WIDTH), BF16))
            out_specs.append(pl.BlockSpec((None, dil, tm // dil, DIL_GROUP_WIDTH),
                                          lambda bi, i: (bi, 0, i, 0)))
    out_shapes.append(jax.ShapeDtypeStruct((b, s, MEM_WIDTH), BF16))
    out_specs.append(tok(MEM_WIDTH))
    return pl.pallas_call(
        _mixproj_kernel,
        grid=(b, s // tm),
        in_specs=in_specs,
        out_specs=out_specs,
        out_shape=out_shapes,
        scratch_shapes=[pltpu.VMEM((3 * (DIL_WIDTH - DIL_GROUP_WIDTH) // LANES, tm, LANES), F32)],
        compiler_params=_params(2),
        name="mixproj",
    )(x1, pos, g, w_b, qn, wuq, kvn, wukv, rope_consts)


def _mla_attn_kernel(q_ref, k_ref, v_ref, o_ref, *, tk, unroll):
    tq = q_ref.shape[0]
    s_len = k_ref.shape[0]
    heads = q_ref.shape[1] // MLA_HEAD_PAD
    blks = [slice(hd * MLA_HEAD_PAD, (hd + 1) * MLA_HEAD_PAD) for hd in range(heads)]

    def body(j, carry):
        rows = pl.ds(pl.multiple_of(j * tk, tk), tk)
        new = []
        for blk, (m, acc) in zip(blks, carry):
            s = _dot_nt(q_ref[:, blk], k_ref[rows, blk])
            m_new = jnp.maximum(m, jnp.max(s, axis=-1, keepdims=True))
            p = jnp.exp2(s - m_new).astype(BF16)
            acc = jnp.exp2(m - m_new) * acc + _dot(p, v_ref[rows, blk])
            new.append((m_new, acc))
        return tuple(new)

    init = tuple((jnp.full((tq, 1), NEG_INF, F32), jnp.zeros((tq, MLA_HEAD_PAD), F32))
                 for _ in range(heads))
    final = lax.fori_loop(0, s_len // tk, body, init, unroll=unroll)
    outs = [acc * (1.0 / acc[:, MLA_V_DIM:MLA_V_DIM + 1]) for _, acc in final]
    lane = lax.broadcasted_iota(jnp.int32, (1, LANES), 1)
    for pr in range(heads // 2):
        even, odd = outs[2 * pr], outs[2 * pr + 1]
        pair = jnp.where(lane < MLA_V_DIM, even, pltpu.roll(odd, MLA_V_DIM, axis=1))
        o_ref[:, pr * LANES:(pr + 1) * LANES] = pair.astype(BF16)


def _mla_attn(q, k, v, *, tq=1024, tk=2048, heads_per_step=4, unroll=2):
    b, s, hw = q.shape
    assert s % tq == 0 and s % tk == 0 and heads_per_step % 2 == 0
    wblk = heads_per_step * MLA_HEAD_PAD
    oblk = heads_per_step * MLA_V_DIM
    return pl.pallas_call(
        functools.partial(_mla_attn_kernel, tk=tk, unroll=unroll),
        grid=(b, hw // wblk, s // tq),
        in_specs=[
            pl.BlockSpec((None, tq, wblk), lambda bi, hi, i: (bi, i, hi)),
            pl.BlockSpec((None, s, wblk), lambda bi, hi, i: (bi, 0, hi)),
            pl.BlockSpec((None, s, wblk), lambda bi, hi, i: (bi, 0, hi)),
        ],
        out_specs=pl.BlockSpec((None, tq, oblk), lambda bi, hi, i: (bi, i, hi)),
        out_shape=jax.ShapeDtypeStruct((b, s, MLA_HEADS * MLA_V_DIM), BF16),
        compiler_params=_params(3),
        name="mla_attn",
    )(q, k, v)


def _dil_attn_kernel(q_ref, k_ref, v_ref, o_ref, lse_ref, *, n_side):
    n_cls, tq_all, _ = q_ref.shape
    length = k_ref.shape[1]
    i = pl.program_id(2)
    lane = lax.broadcasted_iota(jnp.int32, (1, LANES), 1)
    low = lane < DIL_HEAD_DIM
    diff = (lax.broadcasted_iota(jnp.int32, (DIL_Q_TILE, DIL_K_TILE), 1)
            - lax.broadcasted_iota(jnp.int32, (DIL_Q_TILE, DIL_K_TILE), 0))
    ones_blk = jnp.ones((DIL_K_TILE, LANES), BF16)
    for t in range(tq_all // DIL_Q_TILE):
        qs = i * tq_all + t * DIL_Q_TILE
        ks = pl.multiple_of(jnp.clip(qs - n_side, 0, length - DIL_K_TILE), n_side)
        bias = jnp.where(jnp.abs(diff + (ks - qs)) <= n_side, 0.0, NEG_INF)
        bias2 = jnp.concatenate([bias, bias], axis=0)
        rows = slice(t * DIL_Q_TILE, (t + 1) * DIL_Q_TILE)
        for c in range(n_cls):
            for pr in range(DIL_GROUP_WIDTH // LANES):
                cols = slice(pr * LANES, (pr + 1) * LANES)
                qp = q_ref[c, rows, cols]
                kp = k_ref[c, pl.ds(ks, DIL_K_TILE), cols]
                vp = v_ref[c, pl.ds(ks, DIL_K_TILE), cols]
                zero = jnp.zeros_like(qp)
                q2 = jnp.concatenate([jnp.where(low, qp, zero), jnp.where(low, zero, qp)], axis=0)
                s = _dot_nt(q2, kp) + bias2
                m = jnp.max(s, axis=-1, keepdims=True)
                p = jnp.exp2(s - m).astype(BF16)
                pv = _dot(p, jnp.concatenate([vp, ones_blk], axis=1))
                pick = lambda a: jnp.where(low, a[:DIL_Q_TILE], a[DIL_Q_TILE:])
                num, den = pick(pv[:, :LANES]), pick(pv[:, LANES:])
                o_ref[c, rows, cols] = (num * (1.0 / den)).astype(BF16)
                lse_ref[c, rows, cols] = pick(jnp.broadcast_to(m, (2 * DIL_Q_TILE, LANES))) + jnp.log2(den)


def _dil_attn(q, k, v, n_side, *, subtiles_per_step=32):
    b, dil, length, w = q.shape
    assert w == DIL_GROUP_WIDTH and n_side == DIL_Q_TILE // 2
    assert DIL_K_TILE == DIL_Q_TILE + 2 * n_side and length >= DIL_K_TILE
    tq = min(length, subtiles_per_step * DIL_Q_TILE)
    n_cls = min(dil, (subtiles_per_step * DIL_Q_TILE) // tq)
    assert length % tq == 0 and tq % DIL_Q_TILE == 0 and dil % n_cls == 0
    q_spec = pl.BlockSpec((None, n_cls, tq, w), lambda bi, r, i: (bi, r, i, 0))
    kv_spec = pl.BlockSpec((None, n_cls, length, w), lambda bi, r, i: (bi, r, 0, 0))
    return pl.pallas_call(
        functools.partial(_dil_attn_kernel, n_side=n_side),
        grid=(b, dil // n_cls, length // tq),
        in_specs=[q_spec, kv_spec, kv_spec],
        out_specs=[q_spec, q_spec],
        out_shape=[jax.ShapeDtypeStruct(q.shape, BF16), jax.ShapeDtypeStruct(q.shape, F32)],
        compiler_params=_params(3),
        name=f"dil_attn_d{dil}",
    )(q, k, v)


def _mem_kv_kernel(mem_ref, g_ref, w_ref, k_ref, v_ref):
    hm = _rms(mem_ref[...], g_ref[...]).astype(BF16)
    kv = _dot(hm, w_ref[...])
    k_ref[...] = kv[:, :MEM_WIDTH].astype(BF16)
    v_ref[...] = kv[:, MEM_WIDTH:].astype(BF16)


def _mem_kv(mem, g, w_kv):
    b, m, d = mem.shape
    spec_out = pl.BlockSpec((None, m, MEM_WIDTH), lambda bi: (bi, 0, 0))
    return pl.pallas_call(
        _mem_kv_kernel,
        grid=(b,),
        in_specs=[pl.BlockSpec((None, m, d), lambda bi: (bi, 0, 0)),
                  _const_spec(g.shape), _const_spec(w_kv.shape)],
        out_specs=[spec_out, spec_out],
        out_shape=[jax.ShapeDtypeStruct((b, m, MEM_WIDTH), BF16)] * 2,
        compiler_params=_params(1),
        name="mem_kv",
    )(mem, g, w_kv)


def _mix_kernel(x_ref, g_ref, wgate_ref, omla_ref, womla_ref,
                o0_ref, l0_ref, o1_ref, l1_ref, o2_ref, l2_ref, wodil_ref,
                mq_ref, mk_ref, mv_ref, womem_ref, wout_ref,
                out_ref, so1_ref, sl1_ref, so2_ref, sl2_ref):
    x = x_ref[...]
    d = x.shape[1]
    h = _rms(x, g_ref[...]).astype(BF16)

    o_mla = omla_ref[...]
    head_blks = [slice(hd * MEM_HEAD_DIM, (hd + 1) * MEM_HEAD_DIM) for hd in range(MEM_HEADS)]
    scores = [_dot_nt(mq_ref[:, blk], mk_ref[:, blk]) for blk in head_blks]

    col_chunks = [slice(c0, c0 + MXU_COLS) for c0 in range(0, d, MXU_COLS)]

    def gate(branch, cols):
        lo = branch * d + cols.start
        return jax.nn.sigmoid(_dot(h, wgate_ref[:, lo:lo + MXU_COLS]))

    mixed = [gate(0, cols) * _dot(o_mla, womla_ref[:, cols]) for cols in col_chunks]
    gates_dil = [gate(1, cols) for cols in col_chunks]
    gates_mem = [gate(2, cols) for cols in col_chunks]

    o_mem = []
    for blk, s in zip(head_blks, scores):
        m = jnp.max(s, axis=-1, keepdims=True)
        p = jnp.exp(s - m)
        l = jnp.sum(p, axis=-1, keepdims=True)
        o_mem.append((_dot(p.astype(BF16), mv_ref[:, blk]) * (1.0 / l)).astype(BF16))
    o_mem = jnp.concatenate(o_mem, axis=1)

    gblk = DIL_GROUP_WIDTH // LANES
    for o_ref, l_ref, so_ref, sl_ref in ((o1_ref, l1_ref, so1_ref, sl1_ref),
                                         (o2_ref, l2_ref, so2_ref, sl2_ref)):
        dil, n = o_ref.shape[0], o_ref.shape[1]
        for r in range(dil):
            rows = pl.ds(r, n, stride=dil)
            for jj in range(gblk):
                cols = slice(jj * LANES, (jj + 1) * LANES)
                so_ref[jj, rows, :] = o_ref[r, :, cols].astype(F32)
                sl_ref[jj, rows, :] = l_ref[r, :, cols]
    o_parts = []
    for jj in range(gblk):
        cols = slice(jj * LANES, (jj + 1) * LANES)
        l0, l1, l2 = l0_ref[:, cols], sl1_ref[jj], sl2_ref[jj]
        lmax = jnp.maximum(jnp.maximum(l0, l1), l2)
        w0, w1, w2 = jnp.exp2(l0 - lmax), jnp.exp2(l1 - lmax), jnp.exp2(l2 - lmax)
        o_sum = w0 * o0_ref[:, cols].astype(F32) + w1 * so1_ref[jj] + w2 * so2_ref[jj]
        o_parts.append((o_sum * (1.0 / (w0 + w1 + w2))).astype(BF16))
    o_dil = jnp.concatenate(o_parts, axis=1)

    for i, cols in enumerate(col_chunks):
        acc = mixed[i] + gates_dil[i] * _dot(o_dil, wodil_ref[:, cols])
        acc = acc + gates_mem[i] * _dot(o_mem, womem_ref[:, cols])
        mixed[i] = acc.astype(BF16)
    out_ref[...] = x + _dot(jnp.concatenate(mixed, axis=1), wout_ref[...])


def _mix(x1, g, w_gate, o_mla, wo_mla, dil_outs, wo_dil, mq, mem_k, mem_v, wo_mem, w_out, *, tm=512):
    b, s, d = x1.shape
    assert s % tm == 0
    n_mem = mem_k.shape[1]
    tok = lambda width: pl.BlockSpec((None, tm, width), lambda bi, i: (bi, i, 0))
    in_specs = [tok(d), _const_spec(g.shape), _const_spec(w_gate.shape),
                tok(o_mla.shape[2]), _const_spec(wo_mla.shape)]
    args = [x1, g, w_gate, o_mla, wo_mla]
    for (o_g, lse_g), (_, dil) in zip(dil_outs, DIL_GROUPS):
        if dil == 1:
            spec = tok(DIL_GROUP_WIDTH)
            o_g, lse_g = o_g.reshape(b, s, DIL_GROUP_WIDTH), lse_g.reshape(b, s, DIL_GROUP_WIDTH)
        else:
            spec = pl.BlockSpec((None, dil, tm // dil, DIL_GROUP_WIDTH), lambda bi, i: (bi, 0, i, 0))
        in_specs += [spec, spec]
        args += [o_g, lse_g]
    in_specs += [_const_spec(wo_dil.shape), tok(MEM_WIDTH),
                 pl.BlockSpec((None, n_mem, MEM_WIDTH), lambda bi, i: (bi, 0, 0)),
                 pl.BlockSpec((None, n_mem, MEM_WIDTH), lambda bi, i: (bi, 0, 0)),
                 _const_spec(wo_mem.shape), _const_spec(w_out.shape)]
    args += [wo_dil, mq, mem_k, mem_v, wo_mem, w_out]
    return pl.pallas_call(
        _mix_kernel,
        grid=(b, s // tm),
        in_specs=in_specs,
        out_specs=tok(d),
        out_shape=jax.ShapeDtypeStruct((b, s, d), F32),
        scratch_shapes=[pltpu.VMEM((DIL_GROUP_WIDTH // LANES, tm, LANES), F32)] * 4,
        compiler_params=_params(2),
        name="mix",
    )(*args)


def _mla_lane_map():
    half = MLA_ROPE_DIM // 2
    r1 = MLA_ROPE_LANE
    lanes = np.full(LANES, MLA_QK_DIM)
    lanes[r1:r1 + half] = MLA_NOPE_DIM + np.arange(half)
    lanes[r1 + ROPE_PARTNER_SHIFT:r1 + ROPE_PARTNER_SHIFT + half] = MLA_NOPE_DIM + half + np.arange(half)
    free = [i for i in range(MLA_QK_DIM) if lanes[i] == MLA_QK_DIM]
    lanes[free] = np.arange(MLA_NOPE_DIM)
    return lanes


def _rope_consts():
    half_m, half_d = MLA_ROPE_DIM // 2, DIL_ROPE_DIM // 2
    f_m = ROPE_THETA ** (-2.0 * jnp.arange(half_m, dtype=F32) / MLA_ROPE_DIM)
    f_d = ROPE_THETA ** (-2.0 * jnp.arange(half_d, dtype=F32) / DIL_ROPE_DIM)
    assert MLA_ROPE_LANE == 2 * half_d and MLA_ROPE_LANE + half_m <= DIL_HEAD_DIM
    half_row = jnp.concatenate([f_d, f_d, f_m, jnp.zeros((DIL_HEAD_DIM - MLA_ROPE_LANE - half_m,), F32)])
    signs = np.zeros((7, LANES), np.float32)
    for base, sign in ((0, -1.0), (ROPE_PARTNER_SHIFT, 1.0)):
        signs[0, base + MLA_ROPE_LANE:base + MLA_ROPE_LANE + half_m] = sign
        signs[1, base + half_d:base + 2 * half_d] = 1.0
        signs[2, base:base + half_d] = -1.0
    return jnp.concatenate([jnp.concatenate([half_row, half_row])[None, :], jnp.asarray(signs)], axis=0)


def _take_cols(w, idx):
    idx = np.asarray(idx)
    zero = idx < 0
    same_run = np.where(zero[1:] | zero[:-1], zero[1:] & zero[:-1], np.diff(idx) == 1)
    cuts = np.flatnonzero(~same_run) + 1
    parts = []
    for run in np.split(idx, cuts):
        if run[0] < 0:
            parts.append(jnp.zeros(w.shape[:-1] + (len(run),), w.dtype))
        else:
            parts.append(w[..., int(run[0]):int(run[-1]) + 1])
    return jnp.concatenate(parts, axis=-1).astype(BF16)


def _prep_mix_weights(w_in, w_uq, w_ukv):
    n_lat = MLA_Q_RANK + MLA_KV_RANK
    kr0 = n_lat
    dil0 = kr0 + MLA_ROPE_DIM
    mq0 = dil0 + 3 * DIL_WIDTH
    gate0 = mq0 + MEM_WIDTH
    mla_map = _mla_lane_map()
    kr_map = np.where((mla_map >= MLA_NOPE_DIM) & (mla_map < MLA_QK_DIM), kr0 + mla_map - MLA_NOPE_DIM, -1)
    w_b = _take_cols(w_in, np.concatenate([np.arange(n_lat), kr_map, np.arange(dil0, gate0)]))
    w_gate = w_in[:, gate0:].astype(BF16)
    hw = MLA_HEADS * MLA_HEAD_PAD
    uq_map = np.where(mla_map < MLA_QK_DIM, mla_map, -1)
    wuq = _take_cols(w_uq.reshape(MLA_Q_RANK, MLA_HEADS, MLA_QK_DIM), uq_map).reshape(MLA_Q_RANK, hw)
    wkv = w_ukv.reshape(MLA_KV_RANK, MLA_HEADS, MLA_NOPE_DIM + MLA_V_DIM)
    k_map = np.where(mla_map < MLA_NOPE_DIM, mla_map, -1)
    v_lane = np.arange(MLA_HEAD_PAD)
    v_map = np.where(v_lane < MLA_V_DIM, MLA_NOPE_DIM + v_lane, -1)
    wukv = jnp.concatenate([_take_cols(wkv, k_map).reshape(MLA_KV_RANK, hw),
                            _take_cols(wkv, v_map).reshape(MLA_KV_RANK, hw)], axis=1)
    return w_b, w_gate, wuq, wukv


def kernel(x, mem, positions, ffn1_norm, ffn1_w_gate, ffn1_w_up, ffn1_w_down, mix_norm, w_in, mla_q_norm, mla_w_uq, mla_kv_norm, mla_w_ukv, mla_w_o, dil_w_o, mem_norm, mem_w_kv, mem_w_o, w_out, ffn2_norm, ffn2_w_gate, ffn2_w_up, ffn2_w_down, final_norm):
    b, s, d = x.shape
    depth = ffn1_norm.shape[0]
    pos = positions.astype(F32).reshape(b, s // LANES, LANES)
    rope_consts = _rope_consts()
    row = lambda v: v.reshape(1, -1)
    bf = lambda w: w.astype(BF16)
    for l in range(depth):
        x = _ffn(x.reshape(b * s, d), row(ffn1_norm[l]), bf(ffn1_w_gate[l]), bf(ffn1_w_up[l]),
                 bf(ffn1_w_down[l])).reshape(b, s, d)

        w_b, w_gate, wuq, wukv = _prep_mix_weights(w_in[l], mla_w_uq[l], mla_w_ukv[l])
        outs = _mixproj(x, pos, row(mix_norm[l]), w_b, row(mla_q_norm[l]), wuq,
                        row(mla_kv_norm[l]), wukv, rope_consts)
        q_m, k_m, v_m = outs[0:3]
        mq = outs[12]
        o_mla = _mla_attn(q_m, k_m, v_m)
        dil_outs = []
        for g, (window, dil) in enumerate(DIL_GROUPS):
            qg, kg, vg = outs[3 + 3 * g:6 + 3 * g]
            dil_outs.append(_dil_attn(qg, kg, vg, window // (2 * dil)))
        mem_k, mem_v = _mem_kv(mem, row(mem_norm[l]), bf(mem_w_kv[l]))
        x = _mix(x, row(mix_norm[l]), w_gate, o_mla, bf(mla_w_o[l]), dil_outs, bf(dil_w_o[l]),
                 mq, mem_k, mem_v, bf(mem_w_o[l]), bf(w_out[l]))

        last = l == depth - 1
        x = _ffn(x.reshape(b * s, d), row(ffn2_norm[l]), bf(ffn2_w_gate[l]), bf(ffn2_w_up[l]),
                 bf(ffn2_w_down[l]), row(final_norm) if last else None).reshape(b, s, d)
    return x
```

```python
import functools

import numpy as np
import jax
import jax.numpy as jnp
from jax import lax
from jax.experimental import pallas as pl
from jax.experimental.pallas import tpu as pltpu

F32 = jnp.float32
BF16 = jnp.bfloat16

NORM_EPS = 1e-6
NEG_INF = -1e30
ROPE_THETA = 500000.0
LOG2_E = 1.4426950408889634

LANES = 128
MXU_COLS = 256
VMEM_LIMIT_BYTES = 56 * 1024 * 1024

MLA_HEADS = 8
MLA_Q_RANK = 384
MLA_KV_RANK = 128
MLA_NOPE_DIM = 64
MLA_ROPE_DIM = 32
MLA_V_DIM = 64
MLA_QK_DIM = MLA_NOPE_DIM + MLA_ROPE_DIM
MLA_HEAD_PAD = 128
MLA_ROPE_LANE = 16
ROPE_PARTNER_SHIFT = LANES // 2

DIL_GROUPS = ((128, 1), (512, 4), (2048, 16))
DIL_HEADS = 4
DIL_HEAD_DIM = 64
DIL_ROPE_DIM = 16
DIL_GROUP_WIDTH = DIL_HEADS * DIL_HEAD_DIM
DIL_WIDTH = len(DIL_GROUPS) * DIL_GROUP_WIDTH
DIL_Q_TILE = 128
DIL_K_TILE = 256

MEM_HEADS = 4
MEM_HEAD_DIM = 128
MEM_WIDTH = MEM_HEADS * MEM_HEAD_DIM

PROJ_KR0 = MLA_Q_RANK + MLA_KV_RANK
PROJ_DIL0 = PROJ_KR0 + MLA_HEAD_PAD
PROJ_MQ0 = PROJ_DIL0 + 3 * DIL_WIDTH


def _dot(a, b):
    return jnp.dot(a, b, preferred_element_type=F32)


def _dot_nt(a, b):
    return lax.dot_general(a, b, (((1,), (1,)), ((), ())), preferred_element_type=F32)


def _rms(x, g):
    y = x * lax.rsqrt(jnp.mean(x * x, axis=-1, keepdims=True) + NORM_EPS)
    return y * g


def _const_spec(shape):
    nd = len(shape)
    return pl.BlockSpec(shape, lambda *_: (0,) * nd, pipeline_mode=pl.Buffered(1))


def _params(n_grid):
    return pltpu.CompilerParams(
        dimension_semantics=("arbitrary",) * n_grid,
        vmem_limit_bytes=VMEM_LIMIT_BYTES,
    )


def _ffn_kernel(x_ref, g_ref, wg_ref, wu_ref, wd_ref, *rest, f_chunk, final):
    if final:
        fg_ref, o_ref = rest
    else:
        (o_ref,) = rest
    x = x_ref[...]
    h = _rms(x, g_ref[...]).astype(BF16)
    d_ff = wg_ref.shape[1]
    acc = jnp.zeros(x.shape, F32)
    for c in range(d_ff // f_chunk):
        cols = slice(c * f_chunk, (c + 1) * f_chunk)
        a = _dot(h, wg_ref[:, cols])
        b = _dot(h, wu_ref[:, cols])
        act = (jax.nn.silu(a) * b).astype(BF16)
        acc = acc + _dot(act, wd_ref[cols, :])
    y = x + 0.5 * acc
    if final:
        y = _rms(y, fg_ref[...])
    o_ref[...] = y


def _ffn(x2d, g, wg, wu, wd, final_g=None, *, tm=1024, f_chunk=256):
    t, d = x2d.shape
    d_ff = wg.shape[1]
    assert t % tm == 0 and d_ff % f_chunk == 0
    final = final_g is not None
    in_specs = [
        pl.BlockSpec((tm, d), lambda i: (i, 0)),
        _const_spec((1, d)),
        _const_spec((d, d_ff)),
        _const_spec((d, d_ff)),
        _const_spec((d_ff, d)),
    ]
    args = [x2d, g, wg, wu, wd]
    if final:
        in_specs.append(_const_spec((1, d)))
        args.append(final_g)
    return pl.pallas_call(
        functools.partial(_ffn_kernel, f_chunk=f_chunk, final=final),
        grid=(t // tm,),
        in_specs=in_specs,
        out_specs=pl.BlockSpec((tm, d), lambda i: (i, 0)),
        out_shape=jax.ShapeDtypeStruct((t, d), F32),
        compiler_params=_params(1),
        name="ffn_final" if final else "ffn",
    )(*args)


def _rope_apply(x, c, s):
    return x * c + pltpu.roll(x, ROPE_PARTNER_SHIFT, axis=1) * s


def _rope_apply_near(x, c, s_up, s_down):
    half = DIL_ROPE_DIM // 2
    return x * c + pltpu.roll(x, half, axis=1) * s_up + pltpu.roll(x, LANES - half, axis=1) * s_down


def _mixproj_kernel(x_ref, pos_ref, g_ref, w_ref, qn_ref, wuq_ref, kvn_ref, wukv_ref, rc_ref,
                    qm_ref, km_ref, vm_ref,
                    q0_ref, k0_ref, v0_ref, q1_ref, k1_ref, v1_ref, q2_ref, k2_ref, v2_ref,
                    mq_ref, zd_ref):
    tm = x_ref.shape[0]
    h = _rms(x_ref[...], g_ref[...]).astype(BF16)
    rows_per_tile = tm // LANES
    row0 = pl.program_id(1) * rows_per_tile
    pos = jnp.concatenate(
        [jnp.broadcast_to(pos_ref[pl.ds(row0 + r, 1), :], (LANES, LANES)).T for r in range(rows_per_tile)],
        axis=0)
    rc = rc_ref[...]

    ang = pos * rc[0:1]
    cos_t, sin_t = jnp.cos(ang), jnp.sin(ang)
    sgn_m, up_d, down_d = rc[1:2], rc[2:3], rc[3:4]
    c_m, s_m = jnp.where((up_d != 0.0) | (down_d != 0.0), 1.0, cos_t), sin_t * sgn_m
    c_d, su_d, sd_d = jnp.where(sgn_m != 0.0, 1.0, cos_t), sin_t * up_d, sin_t * down_d

    q_scale = MLA_QK_DIM ** -0.5 * LOG2_E
    cq_m, sq_m = c_m * q_scale, s_m * q_scale

    z = _dot(h, w_ref[:, 0:PROJ_DIL0])
    cq = _rms(z[:, 0:MLA_Q_RANK], qn_ref[...]).astype(BF16)
    ckv = _rms(z[:, MLA_Q_RANK:PROJ_KR0], kvn_ref[...]).astype(BF16)
    kr = _rope_apply(z[:, PROJ_KR0:PROJ_DIL0], c_m, s_m)

    for c0 in range(0, MEM_WIDTH, MXU_COLS):
        mq_c = _dot(h, w_ref[:, PROJ_MQ0 + c0:PROJ_MQ0 + c0 + MXU_COLS])
        mq_ref[:, c0:c0 + MXU_COLS] = (mq_c * (MEM_HEAD_DIM ** -0.5)).astype(BF16)

    d_scale = DIL_HEAD_DIM ** -0.5 * LOG2_E
    cq_d, suq_d, sdq_d = c_d * d_scale, su_d * d_scale, sd_d * d_scale
    outs = ((q0_ref, k0_ref, v0_ref), (q1_ref, k1_ref, v1_ref), (q2_ref, k2_ref, v2_ref))
    nblk = DIL_WIDTH // LANES
    gblk = DIL_GROUP_WIDTH // LANES
    for c0 in reversed(range(0, 3 * DIL_WIDTH, MXU_COLS)):
        zd_c = _dot(h, w_ref[:, PROJ_DIL0 + c0:PROJ_DIL0 + c0 + MXU_COLS])
        for o in range(0, MXU_COLS, LANES):
            j = (c0 + o) // LANES
            part, g, jj = j // nblk, (j % nblk) // gblk, j % gblk
            blk = zd_c[:, o:o + LANES]
            if part == 0:
                blk = _rope_apply_near(blk, cq_d, suq_d, sdq_d)
            elif part == 1:
                blk = _rope_apply_near(blk, c_d, su_d, sd_d)
            o_ref, dil = outs[g][part], DIL_GROUPS[g][1]
            cols = slice(jj * LANES, (jj + 1) * LANES)
            if dil == 1:
                o_ref[0, :, cols] = blk.astype(BF16)
            else:
                slab = part * (nblk - gblk) + (g - 1) * gblk + jj
                zd_ref[slab] = blk
                for r in range(dil):
                    o_ref[r, :, cols] = zd_ref[slab, pl.ds(r, tm // dil, stride=dil), :].astype(BF16)

    lane = lax.broadcasted_iota(jnp.int32, (1, LANES), 1)
    ones_col = (lane == MLA_V_DIM).astype(F32)
    hw = MLA_HEADS * MLA_HEAD_PAD
    for c0 in range(0, hw, MXU_COLS):
        q_c = _dot(cq, wuq_ref[:, c0:c0 + MXU_COLS])
        k_c = _dot(ckv, wukv_ref[:, c0:c0 + MXU_COLS])
        v_c = _dot(ckv, wukv_ref[:, hw + c0:hw + c0 + MXU_COLS])
        for o in range(0, MXU_COLS, MLA_HEAD_PAD):
            src, dst = slice(o, o + MLA_HEAD_PAD), slice(c0 + o, c0 + o + MLA_HEAD_PAD)
            qm_ref[:, dst] = _rope_apply(q_c[:, src], cq_m, sq_m).astype(BF16)
            km_ref[:, dst] = (k_c[:, src] + kr).astype(BF16)
            vm_ref[:, dst] = (v_c[:, src] + ones_col).astype(BF16)


def _mixproj(x1, pos, g, w_b, qn, wuq, kvn, wukv, rope_consts, *, tm=1024):
    b, s, d = x1.shape
    assert s % tm == 0 and tm % LANES == 0 and DIL_GROUPS[0][1] == 1
    hw = MLA_HEADS * MLA_HEAD_PAD
    tok = lambda width: pl.BlockSpec((None, tm, width), lambda bi, i: (bi, i, 0))
    in_specs = [
        tok(d),
        pl.BlockSpec((None, s // LANES, LANES), lambda bi, i: (bi, 0, 0)),
        _const_spec(g.shape),
        _const_spec(w_b.shape),
        _const_spec(qn.shape),
        _const_spec(wuq.shape),
        _const_spec(kvn.shape),
        _const_spec(wukv.shape),
        _const_spec(rope_consts.shape),
    ]
    out_shapes = [jax.ShapeDtypeStruct((b, s, hw), BF16)] * 3
    out_specs = [tok(hw)] * 3
    for _, dil in DIL_GROUPS:
        for _ in range(3):
            out_shapes.append(jax.ShapeDtypeStruct((b, dil, s // dil, DIL_GROUP_WIDTH), BF16))
            out_specs.append(pl.BlockSpec((None, dil, tm // dil, DIL_GROUP_WIDTH),
                                          lambda bi, i: (bi, 0, i, 0)))
    out_shapes.append(jax.ShapeDtypeStruct((b, s, MEM_WIDTH), BF16))
    out_specs.append(tok(MEM_WIDTH))
    return pl.pallas_call(
        _mixproj_kernel,
        grid=(b, s // tm),
        in_specs=in_specs,
        out_specs=out_specs,
        out_shape=out_shapes,
        scratch_shapes=[pltpu.VMEM((3 * (DIL_WIDTH - DIL_GROUP_WIDTH) // LANES, tm, LANES), F32)],
        compiler_params=_params(2),
        name="mixproj",
    )(x1, pos, g, w_b, qn, wuq, kvn, wukv, rope_consts)


def _mla_attn_kernel(q_ref, k_ref, v_ref, o_ref, *, tk, unroll):
    tq = q_ref.shape[0]
    s_len = k_ref.shape[0]
    heads = q_ref.shape[1] // MLA_HEAD_PAD
    blks = [slice(hd * MLA_HEAD_PAD, (hd + 1) * MLA_HEAD_PAD) for hd in range(heads)]

    def body(j, carry):
        rows = pl.ds(pl.multiple_of(j * tk, tk), tk)
        new = []
        for blk, (m, acc) in zip(blks, carry):
            s = _dot_nt(q_ref[:, blk], k_ref[rows, blk])
            m_new = jnp.maximum(m, jnp.max(s, axis=-1, keepdims=True))
            p = jnp.exp2(s - m_new).astype(BF16)
            acc = jnp.exp2(m - m_new) * acc + _dot(p, v_ref[rows, blk])
            new.append((m_new, acc))
        return tuple(new)

    init = tuple((jnp.full((tq, 1), NEG_INF, F32), jnp.zeros((tq, MLA_HEAD_PAD), F32))
                 for _ in range(heads))
    final = lax.fori_loop(0, s_len // tk, body, init, unroll=unroll)
    outs = [acc * (1.0 / acc[:, MLA_V_DIM:MLA_V_DIM + 1]) for _, acc in final]
    lane = lax.broadcasted_iota(jnp.int32, (1, LANES), 1)
    for pr in range(heads // 2):
        even, odd = outs[2 * pr], outs[2 * pr + 1]
        pair = jnp.where(lane < MLA_V_DIM, even, pltpu.roll(odd, MLA_V_DIM, axis=1))
        o_ref[:, pr * LANES:(pr + 1) * LANES] = pair.astype(BF16)


def _mla_attn(q, k, v, *, tq=1024, tk=2048, heads_per_step=4, unroll=2):
    b, s, hw = q.shape
    assert s % tq == 0 and s % tk == 0 and heads_per_step % 2 == 0
    wblk = heads_per_step * MLA_HEAD_PAD
    oblk = heads_per_step * MLA_V_DIM
    return pl.pallas_call(
        functools.partial(_mla_attn_kernel, tk=tk, unroll=unroll),
        grid=(b, hw // wblk, s // tq),
        in_specs=[
            pl.BlockSpec((None, tq, wblk), lambda bi, hi, i: (bi, i, hi)),
            pl.BlockSpec((None, s, wblk), lambda bi, hi, i: (bi, 0, hi)),
            pl.BlockSpec((None, s, wblk), lambda bi, hi, i: (bi, 0, hi)),
        ],
        out_specs=pl.BlockSpec((None, tq, oblk), lambda bi, hi, i: (bi, i, hi)),
        out_shape=jax.ShapeDtypeStruct((b, s, MLA_HEADS * MLA_V_DIM), BF16),
        compiler_params=_params(3),
        name="mla_attn",
    )(q, k, v)


def _dil_attn_kernel(q_ref, k_ref, v_ref, o_ref, lse_ref, *, n_side):
    n_cls, tq_all, _ = q_ref.shape
    length = k_ref.shape[1]
    i = pl.program_id(2)
    lane = lax.broadcasted_iota(jnp.int32, (1, LANES), 1)
    low = lane < DIL_HEAD_DIM
    diff = (lax.broadcasted_iota(jnp.int32, (DIL_Q_TILE, DIL_K_TILE), 1)
            - lax.broadcasted_iota(jnp.int32, (DIL_Q_TILE, DIL_K_TILE), 0))
    ones_blk = jnp.ones((DIL_K_TILE, LANES), BF16)
    for t in range(tq_all // DIL_Q_TILE):
        qs = i * tq_all + t * DIL_Q_TILE
        ks = pl.multiple_of(jnp.clip(qs - n_side, 0, length - DIL_K_TILE), n_side)
        bias = jnp.where(jnp.abs(diff + (ks - qs)) <= n_side, 0.0, NEG_INF)
        bias2 = jnp.concatenate([bias, bias], axis=0)
        rows = slice(t * DIL_Q_TILE, (t + 1) * DIL_Q_TILE)
        for c in range(n_cls):
            for pr in range(DIL_GROUP_WIDTH // LANES):
                cols = slice(pr * LANES, (pr + 1) * LANES)
                qp = q_ref[c, rows, cols]
                kp = k_ref[c, pl.ds(ks, DIL_K_TILE), cols]
                vp = v_ref[c, pl.ds(ks, DIL_K_TILE), cols]
                zero = jnp.zeros_like(qp)
                q2 = jnp.concatenate([jnp.where(low, qp, zero), jnp.where(low, zero, qp)], axis=0)
                s = _dot_nt(q2, kp) + bias2
                m = jnp.max(s, axis=-1, keepdims=True)
                p = jnp.exp2(s - m).astype(BF16)
                pv = _dot(p, jnp.concatenate([vp, ones_blk], axis=1))
                pick = lambda a: jnp.where(low, a[:DIL_Q_TILE], a[DIL_Q_TILE:])
                num, den = pick(pv[:, :LANES]), pick(pv[:, LANES:])
                o_ref[c, rows, cols] = (num * (1.0 / den)).astype(BF16)
                lse_ref[c, rows, cols] = pick(jnp.broadcast_to(m, (2 * DIL_Q_TILE, LANES))) + jnp.log2(den)


def _dil_attn(q, k, v, n_side, *, subtiles_per_step=32):
    b, dil, length, w = q.shape
    assert w == DIL_GROUP_WIDTH and n_side == DIL_Q_TILE // 2
    assert DIL_K_TILE == DIL_Q_TILE + 2 * n_side and length >= DIL_K_TILE
    tq = min(length, subtiles_per_step * DIL_Q_TILE)
    n_cls = min(dil, (subtiles_per_step * DIL_Q_TILE) // tq)
    assert length % tq == 0 and tq % DIL_Q_TILE == 0 and dil % n_cls == 0
    q_spec = pl.BlockSpec((None, n_cls, tq, w), lambda bi, r, i: (bi, r, i, 0))
    kv_spec = pl.BlockSpec((None, n_cls, length, w), lambda bi, r, i: (bi, r, 0, 0))
    return pl.pallas_call(
        functools.partial(_dil_attn_kernel, n_side=n_side),
        grid=(b, dil // n_cls, length // tq),
        in_specs=[q_spec, kv_spec, kv_spec],
        out_specs=[q_spec, q_spec],
        out_shape=[jax.ShapeDtypeStruct(q.shape, BF16), jax.ShapeDtypeStruct(q.shape, F32)],
        compiler_params=_params(3),
        name=f"dil_attn_d{dil}",
    )(q, k, v)


def _mem_kv_kernel(mem_ref, g_ref, w_ref, k_ref, v_ref):
    hm = _rms(mem_ref[...], g_ref[...]).astype(BF16)
    kv = _dot(hm, w_ref[...])
    k_ref[...] = kv[:, :MEM_WIDTH].astype(BF16)
    v_ref[...] = kv[:, MEM_WIDTH:].astype(BF16)


def _mem_kv(mem, g, w_kv):
    b, m, d = mem.shape
    spec_out = pl.BlockSpec((None, m, MEM_WIDTH), lambda bi: (bi, 0, 0))
    return pl.pallas_call(
        _mem_kv_kernel,
        grid=(b,),
        in_specs=[pl.BlockSpec((None, m, d), lambda bi: (bi, 0, 0)),
                  _const_spec(g.shape), _const_spec(w_kv.shape)],
        out_specs=[spec_out, spec_out],
        out_shape=[jax.ShapeDtypeStruct((b, m, MEM_WIDTH), BF16)] * 2,
        compiler_params=_params(1),
        name="mem_kv",
    )(mem, g, w_kv)


def _mix_kernel(x_ref, g_ref, wgate_ref, omla_ref, womla_ref,
                o0_ref, l0_ref, o1_ref, l1_ref, o2_ref, l2_ref, wodil_ref,
                mq_ref, mk_ref, mv_ref, womem_ref, wout_ref,
                out_ref, so1_ref, sl1_ref, so2_ref, sl2_ref):
    x = x_ref[...]
    d = x.shape[1]
    h = _rms(x, g_ref[...]).astype(BF16)

    o_mla = omla_ref[...]
    head_blks = [slice(hd * MEM_HEAD_DIM, (hd + 1) * MEM_HEAD_DIM) for hd in range(MEM_HEADS)]
    scores = [_dot_nt(mq_ref[:, blk], mk_ref[:, blk]) for blk in head_blks]

    col_chunks = [slice(c0, c0 + MXU_COLS) for c0 in range(0, d, MXU_COLS)]

    def gate(branch, cols):
        lo = branch * d + cols.start
        return jax.nn.sigmoid(_dot(h, wgate_ref[:, lo:lo + MXU_COLS]))

    mixed = [gate(0, cols) * _dot(o_mla, womla_ref[:, cols]) for cols in col_chunks]
    gates_dil = [gate(1, cols) for cols in col_chunks]
    gates_mem = [gate(2, cols) for cols in col_chunks]

    o_mem = []
    for blk, s in zip(head_blks, scores):
        m = jnp.max(s, axis=-1, keepdims=True)
        p = jnp.exp(s - m)
        l = jnp.sum(p, axis=-1, keepdims=True)
        o_mem.append((_dot(p.astype(BF16), mv_ref[:, blk]) * (1.0 / l)).astype(BF16))
    o_mem = jnp.concatenate(o_mem, axis=1)

    gblk = DIL_GROUP_WIDTH // LANES
    for o_ref, l_ref, so_ref, sl_ref in ((o1_ref, l1_ref, so1_ref, sl1_ref),
                                         (o2_ref, l2_ref, so2_ref, sl2_ref)):
        dil, n = o_ref.shape[0], o_ref.shape[1]
        for r in range(dil):
            rows = pl.ds(r, n, stride=dil)
            for jj in range(gblk):
                cols = slice(jj * LANES, (jj + 1) * LANES)
                so_ref[jj, rows, :] = o_ref[r, :, cols].astype(F32)
                sl_ref[jj, rows, :] = l_ref[r, :, cols]
    o_parts = []
    for jj in range(gblk):
        cols = slice(jj * LANES, (jj + 1) * LANES)
        l0, l1, l2 = l0_ref[:, cols], sl1_ref[jj], sl2_ref[jj]
        lmax = jnp.maximum(jnp.maximum(l0, l1), l2)
        w0, w1, w2 = jnp.exp2(l0 - lmax), jnp.exp2(l1 - lmax), jnp.exp2(l2 - lmax)
        o_sum = w0 * o0_ref[:, cols].astype(F32) + w1 * so1_ref[jj] + w2 * so2_ref[jj]
        o_parts.append((o_sum * (1.0 / (w0 + w1 + w2))).astype(BF16))
    o_dil = jnp.concatenate(o_parts, axis=1)

    for i, cols in enumerate(col_chunks):
        acc = mixed[i] + gates_dil[i] * _dot(o_dil, wodil_ref[:, cols])
        acc = acc + gates_mem[i] * _dot(o_mem, womem_ref[:, cols])
        mixed[i] = acc.astype(BF16)
    out_ref[...] = x + _dot(jnp.concatenate(mixed, axis=1), wout_ref[...])


def _mix(x1, g, w_gate, o_mla, wo_mla, dil_outs, wo_dil, mq, mem_k, mem_v, wo_mem, w_out, *, tm=512):
    b, s, d = x1.shape
    assert s % tm == 0
    n_mem = mem_k.shape[1]
    tok = lambda width: pl.BlockSpec((None, tm, width), lambda bi, i: (bi, i, 0))
    in_specs = [tok(d), _const_spec(g.shape), _const_spec(w_gate.shape),
                tok(o_mla.shape[2]), _const_spec(wo_mla.shape)]
    args = [x1, g, w_gate, o_mla, wo_mla]
    for (o_g, lse_g), (_, dil) in zip(dil_outs, DIL_GROUPS):
        if dil == 1:
            spec = tok(DIL_GROUP_WIDTH)
            o_g, lse_g = o_g.reshape(b, s, DIL_GROUP_WIDTH), lse_g.reshape(b, s, DIL_GROUP_WIDTH)
        else:
            spec = pl.BlockSpec((None, dil, tm // dil, DIL_GROUP_WIDTH), lambda bi, i: (bi, 0, i, 0))
        in_specs += [spec, spec]
        args += [o_g, lse_g]
    in_specs += [_const_spec(wo_dil.shape), tok(MEM_WIDTH),
                 pl.BlockSpec((None, n_mem, MEM_WIDTH), lambda bi, i: (bi, 0, 0)),
                 pl.BlockSpec((None, n_mem, MEM_WIDTH), lambda bi, i: (bi, 0, 0)),
                 _const_spec(wo_mem.shape), _const_spec(w_out.shape)]
    args += [wo_dil, mq, mem_k, mem_v, wo_mem, w_out]
    return pl.pallas_call(
        _mix_kernel,
        grid=(b, s // tm),
        in_specs=in_specs,
        out_specs=tok(d),
        out_shape=jax.ShapeDtypeStruct((b, s, d), F32),
        scratch_shapes=[pltpu.VMEM((DIL_GROUP_WIDTH // LANES, tm, LANES), F32)] * 4,
        compiler_params=_params(2),
        name="mix",
    )(*args)


def _mla_lane_map():
    half = MLA_ROPE_DIM // 2
    r1 = MLA_ROPE_LANE
    lanes = np.full(LANES, MLA_QK_DIM)
    lanes[r1:r1 + half] = MLA_NOPE_DIM + np.arange(half)
    lanes[r1 + ROPE_PARTNER_SHIFT:r1 + ROPE_PARTNER_SHIFT + half] = MLA_NOPE_DIM + half + np.arange(half)
    free = [i for i in range(MLA_QK_DIM) if lanes[i] == MLA_QK_DIM]
    lanes[free] = np.arange(MLA_NOPE_DIM)
    return lanes


def _rope_consts():
    half_m, half_d = MLA_ROPE_DIM // 2, DIL_ROPE_DIM // 2
    f_m = ROPE_THETA ** (-2.0 * jnp.arange(half_m, dtype=F32) / MLA_ROPE_DIM)
    f_d = ROPE_THETA ** (-2.0 * jnp.arange(half_d, dtype=F32) / DIL_ROPE_DIM)
    assert MLA_ROPE_LANE == 2 * half_d and MLA_ROPE_LANE + half_m <= DIL_HEAD_DIM
    half_row = jnp.concatenate([f_d, f_d, f_m, jnp.zeros((DIL_HEAD_DIM - MLA_ROPE_LANE - half_m,), F32)])
    signs = np.zeros((7, LANES), np.float32)
    for base, sign in ((0, -1.0), (ROPE_PARTNER_SHIFT, 1.0)):
        signs[0, base + MLA_ROPE_LANE:base + MLA_ROPE_LANE + half_m] = sign
        signs[1, base + half_d:base + 2 * half_d] = 1.0
        signs[2, base:base + half_d] = -1.0
    return jnp.concatenate([jnp.concatenate([half_row, half_row])[None, :], jnp.asarray(signs)], axis=0)


def _take_cols(w, idx):
    idx = np.asarray(idx)
    zero = idx < 0
    same_run = np.where(zero[1:] | zero[:-1], zero[1:] & zero[:-1], np.diff(idx) == 1)
    cuts = np.flatnonzero(~same_run) + 1
    parts = []
    for run in np.split(idx, cuts):
        if run[0] < 0:
            parts.append(jnp.zeros(w.shape[:-1] + (len(run),), w.dtype))
        else:
            parts.append(w[..., int(run[0]):int(run[-1]) + 1])
    return jnp.concatenate(parts, axis=-1).astype(BF16)


def _prep_mix_weights(w_in, w_uq, w_ukv):
    n_lat = MLA_Q_RANK + MLA_KV_RANK
    kr0 = n_lat
    dil0 = kr0 + MLA_ROPE_DIM
    mq0 = dil0 + 3 * DIL_WIDTH
    gate0 = mq0 + MEM_WIDTH
    mla_map = _mla_lane_map()
    kr_map = np.where((mla_map >= MLA_NOPE_DIM) & (mla_map < MLA_QK_DIM), kr0 + mla_map - MLA_NOPE_DIM, -1)
    w_b = _take_cols(w_in, np.concatenate([np.arange(n_lat), kr_map, np.arange(dil0, gate0)]))
    w_gate = w_in[:, gate0:].astype(BF16)
    hw = MLA_HEADS * MLA_HEAD_PAD
    uq_map = np.where(mla_map < MLA_QK_DIM, mla_map, -1)
    wuq = _take_cols(w_uq.reshape(MLA_Q_RANK, MLA_HEADS, MLA_QK_DIM), uq_map).reshape(MLA_Q_RANK, hw)
    wkv = w_ukv.reshape(MLA_KV_RANK, MLA_HEADS, MLA_NOPE_DIM + MLA_V_DIM)
    k_map = np.where(mla_map < MLA_NOPE_DIM, mla_map, -1)
    v_lane = np.arange(MLA_HEAD_PAD)
    v_map = np.where(v_lane < MLA_V_DIM, MLA_NOPE_DIM + v_lane, -1)
    wukv = jnp.concatenate([_take_cols(wkv, k_map).reshape(MLA_KV_RANK, hw),
                            _take_cols(wkv, v_map).reshape(MLA_KV_RANK, hw)], axis=1)
    return w_b, w_gate, wuq, wukv


def kernel(x, mem, positions, ffn1_norm, ffn1_w_gate, ffn1_w_up, ffn1_w_down, mix_norm, w_in, mla_q_norm, mla_w_uq, mla_kv_norm, mla_w_ukv, mla_w_o, dil_w_o, mem_norm, mem_w_kv, mem_w_o, w_out, ffn2_norm, ffn2_w_gate, ffn2_w_up, ffn2_w_down, final_norm):
    b, s, d = x.shape
    depth = ffn1_norm.shape[0]
    pos = positions.astype(F32).reshape(b, s // LANES, LANES)
    rope_consts = _rope_consts()
    row = lambda v: v.reshape(1, -1)
    bf = lambda w: w.astype(BF16)
    for l in range(depth):
        x = _ffn(x.reshape(b * s, d), row(ffn1_norm[l]), bf(ffn1_w_gate[l]), bf(ffn1_w_up[l]),
                 bf(ffn1_w_down[l])).reshape(b, s, d)

        w_b, w_gate, wuq, wukv = _prep_mix_weights(w_in[l], mla_w_uq[l], mla_w_ukv[l])
        outs = _mixproj(x, pos, row(mix_norm[l]), w_b, row(mla_q_norm[l]), wuq,
                        row(mla_kv_norm[l]), wukv, rope_consts)
        q_m, k_m, v_m = outs[0:3]
        mq = outs[12]
        o_mla = _mla_attn(q_m, k_m, v_m)
        dil_outs = []
        for g, (window, dil) in enumerate(DIL_GROUPS):
            qg, kg, vg = outs[3 + 3 * g:6 + 3 * g]
            dil_outs.append(_dil_attn(qg, kg, vg, window // (2 * dil)))
        mem_k, mem_v = _mem_kv(mem, row(mem_norm[l]), bf(mem_w_kv[l]))
        x = _mix(x, row(mix_norm[l]), w_gate, o_mla, bf(mla_w_o[l]), dil_outs, bf(dil_w_o[l]),
                 mq, mem_k, mem_v, bf(mem_w_o[l]), bf(w_out[l]))

        last = l == depth - 1
        x = _ffn(x.reshape(b * s, d), row(ffn2_norm[l]), bf(ffn2_w_gate[l]), bf(ffn2_w_up[l]),
                 bf(ffn2_w_down[l]), row(final_norm) if last else None).reshape(b, s, d)
    return x
```
